```python
import math
import jax, jax.numpy as jnp
from jax import lax
import numpy as np

D_MODEL = 1024
BATCH = 8
SEQ = 2048
DEPTH = 2

HEAD_DIM = 64
FOX_HEADS = 4
SB_HEADS = 4
MLA_HEADS = 4
MLA_Q_RANK = 256
MLA_KV_RANK = 128
MLA_NOPE_DIM = 64
MLA_ROPE_DIM = 32
MLA_V_DIM = 128
ROPE_THETA = 10000.0
BLOCK_Q = 128
FOX_W = FOX_HEADS * HEAD_DIM
SB_W = SB_HEADS * HEAD_DIM
MLA_W = MLA_HEADS * MLA_V_DIM
MIX_W = FOX_W + SB_W + MLA_W
IN_SPLITS = (FOX_W, FOX_W, FOX_W, FOX_HEADS, SB_W, SB_W, SB_W, MLA_Q_RANK, MLA_KV_RANK, MLA_ROPE_DIM)
IN_W = sum(IN_SPLITS)

PEER_HEADS = 8
PEER_NKEYS = 128
PEER_EXPERTS = PEER_NKEYS * PEER_NKEYS
PEER_QDIM = 128
PEER_TOPK = 16
TOKEN_CHUNK = 128

PLE_DIM = 256
EPS = 1e-6

kernel_name = "hymba_fox_stickbreak_mla_peer_trunk"


def rms_norm(x, g):
    xf = x.astype(jnp.float32)
    y = xf * lax.rsqrt(jnp.mean(xf * xf, axis=-1, keepdims=True) + EPS)
    return (y * g.astype(jnp.float32)).astype(x.dtype)


def apply_rope(x, pos):
    half = MLA_ROPE_DIM // 2
    inv_freq = ROPE_THETA ** (-jnp.arange(half, dtype=jnp.float32) / half)
    ang = pos.astype(jnp.float32)[..., None] * inv_freq
    ang = ang.reshape(ang.shape[:2] + (1,) * (x.ndim - 3) + (half,))
    cos, sin = jnp.cos(ang), jnp.sin(ang)
    xf = x.astype(jnp.float32)
    x1, x2 = xf[..., :half], xf[..., half:]
    return jnp.concatenate([x1 * cos - x2 * sin, x1 * sin + x2 * cos], axis=-1).astype(x.dtype)


def _split_heads(t, n_heads):
    b, s, _ = t.shape
    return t.reshape(b, s, n_heads, -1).transpose(0, 2, 1, 3)


def _merge_blocks(out):
    nb, b, h, bq, d = out.shape
    return out.transpose(1, 0, 3, 2, 4).reshape(b, nb * bq, h * d)


def causal_softmax_attention(q, k, v, scale, cum_log_f=None):
    s_len = q.shape[2]
    kpos = jnp.arange(s_len)

    def block(i):
        start = i * BLOCK_Q
        qb = lax.dynamic_slice_in_dim(q, start, BLOCK_Q, axis=2)
        qpos = start + jnp.arange(BLOCK_Q)
        logits = jnp.einsum('bhqd,bhkd->bhqk', qb, k, preferred_element_type=jnp.float32) * scale
        if cum_log_f is not None:
            fq = lax.dynamic_slice_in_dim(cum_log_f, start, BLOCK_Q, axis=2)
            logits = logits + fq[..., :, None] - cum_log_f[..., None, :]
        logits = jnp.where(kpos[None, :] <= qpos[:, None], logits, -jnp.inf)
        w = jax.nn.softmax(logits, axis=-1)
        return jnp.einsum('bhqk,bhkd->bhqd', w.astype(v.dtype), v)

    return _merge_blocks(lax.map(block, jnp.arange(s_len // BLOCK_Q)))


def stick_breaking_attention(q, k, v):
    s_len = q.shape[2]
    kpos = jnp.arange(s_len)
    scale = HEAD_DIM ** -0.5

    def block(i):
        start = i * BLOCK_Q
        qb = lax.dynamic_slice_in_dim(q, start, BLOCK_Q, axis=2)
        qpos = start + jnp.arange(BLOCK_Q)
        z = jnp.einsum('bhqd,bhkd->bhqk', qb, k, preferred_element_type=jnp.float32) * scale
        strict = kpos[None, :] < qpos[:, None]
        log_beta = jax.nn.log_sigmoid(z)
        log_one_minus = jnp.where(strict, jax.nn.log_sigmoid(-z), 0.0)
        rest = lax.cumsum(log_one_minus, axis=3, reverse=True) - log_one_minus
        w = jnp.where(strict, jnp.exp(log_beta + rest), 0.0)
        return jnp.einsum('bhqk,bhkd->bhqd', w.astype(v.dtype), v)

    return _merge_blocks(lax.map(block, jnp.arange(s_len // BLOCK_Q)))


def mla_attention(c_q, c_kv, k_rope, pos, q_norm_g, w_uq, kv_norm_g, w_ukv):
    b, s, _ = c_q.shape
    q = (rms_norm(c_q, q_norm_g) @ w_uq).reshape(b, s, MLA_HEADS, MLA_NOPE_DIM + MLA_ROPE_DIM)
    q_nope, q_rot = q[..., :MLA_NOPE_DIM], q[..., MLA_NOPE_DIM:]
    q_rot = apply_rope(q_rot, pos)
    kv = (rms_norm(c_kv, kv_norm_g) @ w_ukv).reshape(b, s, MLA_HEADS, MLA_NOPE_DIM + MLA_V_DIM)
    k_nope, v = kv[..., :MLA_NOPE_DIM], kv[..., MLA_NOPE_DIM:]
    k_rot = jnp.broadcast_to(apply_rope(k_rope, pos)[:, :, None, :], (b, s, MLA_HEADS, MLA_ROPE_DIM))
    q_full = jnp.concatenate([q_nope, q_rot], axis=-1).transpose(0, 2, 1, 3)
    k_full = jnp.concatenate([k_nope, k_rot], axis=-1).transpose(0, 2, 1, 3)
    v = v.transpose(0, 2, 1, 3)
    return causal_softmax_attention(q_full, k_full, v, (MLA_NOPE_DIM + MLA_ROPE_DIM) ** -0.5)


def token_mixer(h, pos, w_in, b_forget, q_norm_g, w_uq, kv_norm_g, w_ukv, out_norm_g, w_o):
    proj = h @ w_in
    offsets = [int(o) for o in np.cumsum(IN_SPLITS)[:-1]]
    fq, fk, fv, f_logit, sq, sk, sv, c_q, c_kv, k_rope = jnp.split(proj, offsets, axis=-1)
    log_f = jax.nn.log_sigmoid(f_logit.astype(jnp.float32) + b_forget.astype(jnp.float32))
    cum_log_f = jnp.cumsum(log_f, axis=1).transpose(0, 2, 1)
    y_fox = causal_softmax_attention(_split_heads(fq, FOX_HEADS), _split_heads(fk, FOX_HEADS),
                                     _split_heads(fv, FOX_HEADS), HEAD_DIM ** -0.5, cum_log_f)
    y_sb = stick_breaking_attention(_split_heads(sq, SB_HEADS), _split_heads(sk, SB_HEADS),
                                    _split_heads(sv, SB_HEADS))
    y_mla = mla_attention(c_q, c_kv, k_rope, pos, q_norm_g, w_uq, kv_norm_g, w_ukv)
    g_fox, g_sb, g_mla = jnp.split(out_norm_g, [FOX_W, FOX_W + SB_W])
    y = jnp.concatenate([rms_norm(y_fox, g_fox), rms_norm(y_sb, g_sb), rms_norm(y_mla, g_mla)], axis=-1)
    return y @ w_o


def peer_ffn(h, w_query, sub_keys, expert_u, expert_v):
    b, s, d = h.shape
    n_chunks = (b * s) // TOKEN_CHUNK
    half = PEER_QDIM // 2

    def chunk(xc):
        c = xc.shape[0]
        q = (xc @ w_query).reshape(c, PEER_HEADS, 2, half)
        scores = jnp.einsum('chpd,hpnd->chpn', q, sub_keys, preferred_element_type=jnp.float32)
        s1, i1 = lax.top_k(scores[:, :, 0], PEER_TOPK)
        s2, i2 = lax.top_k(scores[:, :, 1], PEER_TOPK)
        cand_s = (s1[..., :, None] + s2[..., None, :]).reshape(c, PEER_HEADS, PEER_TOPK * PEER_TOPK)
        cand_id = (i1[..., :, None] * PEER_NKEYS + i2[..., None, :]).reshape(c, PEER_HEADS, PEER_TOPK * PEER_TOPK)
        top_s, top_j = lax.top_k(cand_s, PEER_TOPK)
        ids = jnp.take_along_axis(cand_id, top_j, axis=-1)
        gate = jax.nn.softmax(top_s, axis=-1)
        u = expert_u[ids]
        pre = jnp.einsum('chkd,cd->chk', u, xc, preferred_element_type=jnp.float32)
        act = gate * jax.nn.gelu(pre, approximate=False)
        return jnp.einsum('chk,chkd->cd', act.astype(expert_v.dtype), expert_v[ids])

    out = lax.map(chunk, h.reshape(n_chunks, TOKEN_CHUNK, d))
    return out.reshape(b, s, d)


def per_layer_embedding(h, p_i, w_ple, ple_norm_g, w_ple_gate):
    gate = jax.nn.sigmoid((rms_norm(h, ple_norm_g) @ w_ple_gate).astype(jnp.float32))
    return (gate * (p_i @ w_ple).astype(jnp.float32)).astype(h.dtype)


def setup_inputs(seed: int = 0) -> dict:
    key = jax.random.key(seed)
    ks = jax.random.split(key, 24)
    f32 = jnp.float32

    def normal(k, shape, scale):
        return jax.random.normal(k, shape, f32) * scale

    def gain(k, shape):
        return 1.0 + 0.02 * jax.random.normal(k, shape, f32)

    return {
        "x": normal(ks[0], (BATCH, SEQ, D_MODEL), 1.0),
        "p": normal(ks[1], (DEPTH, BATCH, SEQ, PLE_DIM), 1.0),
        "positions": jnp.broadcast_to(jnp.arange(SEQ, dtype=jnp.int32), (BATCH, SEQ)),
        "norm_mix_g": gain(ks[2], (DEPTH, D_MODEL)),
        "w_in": normal(ks[3], (DEPTH, D_MODEL, IN_W), D_MODEL ** -0.5),
        "b_forget": jax.random.uniform(ks[4], (DEPTH, FOX_HEADS), f32, 1.0, 5.0),
        "mla_q_norm_g": gain(ks[5], (DEPTH, MLA_Q_RANK)),
        "w_uq": normal(ks[6], (DEPTH, MLA_Q_RANK, MLA_HEADS * (MLA_NOPE_DIM + MLA_ROPE_DIM)), MLA_Q_RANK ** -0.5),
        "mla_kv_norm_g": gain(ks[7], (DEPTH, MLA_KV_RANK)),
        "w_ukv": normal(ks[8], (DEPTH, MLA_KV_RANK, MLA_HEADS * (MLA_NOPE_DIM + MLA_V_DIM)), MLA_KV_RANK ** -0.5),
        "mix_out_norm_g": gain(ks[9], (DEPTH, MIX_W)),
        "w_o": normal(ks[10], (DEPTH, MIX_W, D_MODEL), MIX_W ** -0.5),
        "norm_ffn_g": gain(ks[11], (DEPTH, D_MODEL)),
        "peer_w_query": normal(ks[12], (DEPTH, D_MODEL, PEER_HEADS * PEER_QDIM), D_MODEL ** -0.5),
        "peer_sub_keys": normal(ks[13], (DEPTH, PEER_HEADS, 2, PEER_NKEYS, PEER_QDIM // 2), (PEER_QDIM // 2) ** -0.5),
        "peer_u": normal(ks[14], (DEPTH, PEER_EXPERTS, D_MODEL), D_MODEL ** -0.5),
        "peer_v": normal(ks[15], (DEPTH, PEER_EXPERTS, D_MODEL), PEER_HEADS ** -0.5),
        "w_ple": normal(ks[16], (DEPTH, PLE_DIM, D_MODEL), PLE_DIM ** -0.5),
        "ple_norm_g": gain(ks[17], (DEPTH, D_MODEL)),
        "w_ple_gate": normal(ks[18], (DEPTH, D_MODEL, D_MODEL), D_MODEL ** -0.5),
        "final_norm_g": gain(ks[19], (D_MODEL,)),
    }


def reference(x, p, positions, norm_mix_g, w_in, b_forget, mla_q_norm_g, w_uq, mla_kv_norm_g, w_ukv,
              mix_out_norm_g, w_o, norm_ffn_g, peer_w_query, peer_sub_keys, peer_u, peer_v,
              w_ple, ple_norm_g, w_ple_gate, final_norm_g):
    h = x
    for i in range(DEPTH):
        h = h + token_mixer(rms_norm(h, norm_mix_g[i]), positions, w_in[i], b_forget[i],
                            mla_q_norm_g[i], w_uq[i], mla_kv_norm_g[i], w_ukv[i],
                            mix_out_norm_g[i], w_o[i])
        h = h + peer_ffn(rms_norm(h, norm_ffn_g[i]), peer_w_query[i], peer_sub_keys[i], peer_u[i], peer_v[i])
        h = h + per_layer_embedding(h, p[i], w_ple[i], ple_norm_g[i], w_ple_gate[i])
    return rms_norm(h, final_norm_g)
```

```python
import functools
import math

import numpy as np
import jax
import jax.numpy as jnp
from jax import lax
from jax.experimental import pallas as pl
from jax.experimental.pallas import tpu as pltpu

F32 = jnp.float32
BF16 = jnp.bfloat16

D_MODEL = 1024
HEAD_DIM = 64
N_HEADS = 4
MLA_Q_RANK = 256
MLA_KV_RANK = 128
MLA_NOPE = 64
MLA_ROPE = 32
MLA_V = 128
ROPE_THETA = 10000.0
IN_SPLITS = (256, 256, 256, 4, 256, 256, 256, MLA_Q_RANK, MLA_KV_RANK, MLA_ROPE)
PEER_HEADS = 8
PEER_NKEYS = 128
PEER_HALF = 64
PEER_TOPK = 16
PLE_DIM = 256
EPS = 1e-6

LANES = 128
SLOT = LANES
VMEM_LIMIT = 56 * 1024 * 1024

TM_PROJ = 512
TQ_SOFTMAX = 256
TQ_SB = 256
TK_SB = 128
T_PEER = 512
EB_PEER = 512

OB_FQ, OB_FK, OB_FV, OB_SQ, OB_SK, OB_SV, OB_W = 0, 512, 1024, 1280, 1792, 2304, 2560
OF_CQ, OF_CKV, OF_KRA, OF_KRB, OF_FL, OF_W = 0, 256, 384, 512, 640, 768

PEER_PAIRS = tuple((a, b) for a in range(PEER_TOPK) for b in range(PEER_TOPK) if (a + 1) * (b + 1) <= PEER_TOPK)
N_CAND = len(PEER_PAIRS)
N_CAND_PAD = 56


def _cparams(sem):
    return pltpu.CompilerParams(dimension_semantics=sem, vmem_limit_bytes=VMEM_LIMIT)


def _rms(x, g):
    return x * lax.rsqrt(jnp.mean(x * x, axis=-1, keepdims=True) + EPS) * g


def _dot(a, b):
    return jnp.dot(a, b, preferred_element_type=F32)


def _dot_nt(a, b):
    return lax.dot_general(a, b, (((1,), (1,)), ((), ())), preferred_element_type=F32)


def _dot_hi(a, b):
    return jnp.dot(a, b, preferred_element_type=F32, precision=lax.Precision.HIGHEST)


def _inproj_kernel(x_ref, g_ref, wb_ref, wf_ref, ob_ref, of_ref):
    xb = _rms(x_ref[...], g_ref[...]).astype(BF16)
    ob_ref[...] = _dot(xb, wb_ref[...]).astype(BF16)
    of_ref[...] = _dot(xb, wf_ref[...])


def _inproj(h, g, wb, wf):
    n = h.shape[0]
    tm = TM_PROJ
    return pl.pallas_call(
        _inproj_kernel,
        grid=(n // tm,),
        in_specs=[
            pl.BlockSpec((tm, D_MODEL), lambda i: (i, 0)),
            pl.BlockSpec((1, D_MODEL), lambda i: (0, 0)),
            pl.BlockSpec((D_MODEL, OB_W), lambda i: (0, 0)),
            pl.BlockSpec((D_MODEL, OF_W), lambda i: (0, 0)),
        ],
        out_specs=[
            pl.BlockSpec((tm, OB_W), lambda i: (i, 0)),
            pl.BlockSpec((tm, OF_W), lambda i: (i, 0)),
        ],
        out_shape=[jax.ShapeDtypeStruct((n, OB_W), BF16), jax.ShapeDtypeStruct((n, OF_W), F32)],
        compiler_params=_cparams(("parallel",)),
        name="inproj",
    )(h, g, wb, wf)


def _split3(x):
    hi = x.astype(BF16)
    r = x - hi.astype(F32)
    mid = r.astype(BF16)
    lo = (r - mid.astype(F32)).astype(BF16)
    return hi, mid, lo


def _foxprep_kernel(qk_ref, fl_ref, b_ref, o_ref, f_ref, *, rb):
    s_len = fl_ref.shape[0]
    x = fl_ref[...] + b_ref[...]
    logf = jnp.minimum(x, 0.0) - jnp.log(1.0 + jnp.exp(-jnp.abs(x)))
    hi, mid, lo = _split3(logf)
    for r in range(s_len // rb):
        row = lax.broadcasted_iota(jnp.int32, (rb, s_len), 0) + r * rb
        col = lax.broadcasted_iota(jnp.int32, (rb, s_len), 1)
        tri = jnp.where(col <= row, 1.0, 0.0).astype(BF16)
        f_ref[r * rb:(r + 1) * rb, :] = _dot(tri, hi) + _dot(tri, mid) + _dot(tri, lo)
    f = f_ref[...]
    lane = lax.broadcasted_iota(jnp.int32, (s_len, LANES), 1)
    for h in range(N_HEADS):
        fh = f[:, h:h + 1]
        fhi, fmid, flo = (t.astype(F32) for t in _split3(fh))
        qa = jnp.where(lane == 64, fhi, jnp.where(lane == 65, fmid, jnp.where(lane == 66, flo,
             jnp.where((lane >= 67) & (lane < 70), 1.0, 0.0))))
        ka = jnp.where(lane == 67, -fhi, jnp.where(lane == 68, -fmid, jnp.where(lane == 69, -flo,
             jnp.where((lane >= 64) & (lane < 67), 1.0, 0.0))))
        qs = slice(h * SLOT, (h + 1) * SLOT)
        ks = slice(N_HEADS * SLOT + h * SLOT, N_HEADS * SLOT + (h + 1) * SLOT)
        o_ref[:, qs] = (qk_ref[:, qs].astype(F32) + qa).astype(BF16)
        o_ref[:, ks] = (qk_ref[:, ks].astype(F32) + ka).astype(BF16)


def _foxprep(ob, of, bias, batch, s_len):
    n = ob.shape[0]
    w = 2 * N_HEADS * SLOT
    return pl.pallas_call(
        functools.partial(_foxprep_kernel, rb=256),
        grid=(batch,),
        in_specs=[
            pl.BlockSpec((s_len, w), lambda b: (b, 0)),
            pl.BlockSpec((s_len, LANES), lambda b: (b, OF_FL // LANES)),
            pl.BlockSpec((1, LANES), lambda b: (0, 0)),
        ],
        out_specs=pl.BlockSpec((s_len, w), lambda b: (b, 0)),
        out_shape=jax.ShapeDtypeStruct((n, w), BF16),
        scratch_shapes=[pltpu.VMEM((s_len, LANES), F32)],
        compiler_params=_cparams(("parallel",)),
        name="foxprep",
    )(ob, of, bias)


def _softmax_attn_kernel(q_ref, k_ref, v_ref, o_ref, *, tq, pair_sum):
    qi = pl.program_id(2)
    row = lax.broadcasted_iota(jnp.int32, (tq, tq), 0)
    col = lax.broadcasted_iota(jnp.int32, (tq, tq), 1)
    causal = col <= row
    outs = []
    for hh in range(2):
        hs = slice(hh * SLOT, (hh + 1) * SLOT)
        vs = slice(0, SLOT) if pair_sum else hs
        q = q_ref[:, hs]

        def block(j, carry, masked):
            m, l, acc = carry
            kb = k_ref[pl.ds(pl.multiple_of(j * tq, tq), tq), hs]
            vb = v_ref[pl.ds(pl.multiple_of(j * tq, tq), tq), vs]
            s = _dot_nt(q, kb)
            if masked:
                s = jnp.where(causal, s, -1e30)
            m_new = jnp.maximum(m, jnp.max(s, axis=1, keepdims=True))
            alpha = jnp.exp(m - m_new)
            p = jnp.exp(s - m_new)
            l = alpha * l + jnp.sum(p, axis=1, keepdims=True)
            acc = alpha * acc + _dot(p.astype(BF16), vb)
            return m_new, l, acc

        init = (jnp.full((tq, 1), -1e30, F32), jnp.zeros((tq, 1), F32), jnp.zeros((tq, SLOT), F32))
        carry = lax.fori_loop(0, qi, lambda j, c: block(j, c, False), init)
        _, l, acc = block(qi, carry, True)
        outs.append(acc / l)
    if pair_sum:
        lane = lax.broadcasted_iota(jnp.int32, (tq, SLOT), 1)
        o_ref[...] = jnp.where(lane < HEAD_DIM, outs[0], outs[1])
    else:
        o_ref[:, 0:SLOT] = outs[0]
        o_ref[:, SLOT:2 * SLOT] = outs[1]


def _softmax_attn(q_arr, q_col, k_arr, k_col, v_arr, v_col, batch, s_len, pair_sum):
    n = q_arr.shape[0]
    tq = TQ_SOFTMAX
    nq = s_len // tq
    vw = SLOT if pair_sum else 2 * SLOT
    ow = vw
    return pl.pallas_call(
        functools.partial(_softmax_attn_kernel, tq=tq, pair_sum=pair_sum),
        grid=(batch, N_HEADS // 2, nq),
        in_specs=[
            pl.BlockSpec((tq, 2 * SLOT), lambda b, p, i: (b * nq + i, q_col + p)),
            pl.BlockSpec((s_len, 2 * SLOT), lambda b, p, i: (b, k_col + p)),
            pl.BlockSpec((s_len, vw), lambda b, p, i: (b, v_col + p)),
        ],
        out_specs=pl.BlockSpec((tq, ow), lambda b, p, i: (b * nq + i, p)),
        out_shape=jax.ShapeDtypeStruct((n, (N_HEADS // 2) * ow), F32),
        compiler_params=_cparams(("parallel", "parallel", "arbitrary")),
        name="fox_attn" if pair_sum else "mla_attn",
    )(q_arr, k_arr, v_arr)


def _sb_attn_kernel(q_ref, k_ref, v_ref, o_ref, *, tq, tk):
    qi = pl.program_id(2)
    nz = tq // tk
    row = lax.broadcasted_iota(jnp.int32, (tq, tk), 0)
    col = lax.broadcasted_iota(jnp.int32, (tq, tk), 1)
    ur = lax.broadcasted_iota(jnp.int32, (tk, 2 * tk), 0)
    uc = lax.broadcasted_iota(jnp.int32, (tk, 2 * tk), 1)
    uo = jnp.where((uc >= tk) | (ur > uc), 1.0, 0.0).astype(BF16)
    outs = []
    for hh in range(2):
        hs = slice(hh * SLOT, (hh + 1) * SLOT)
        q = q_ref[:, hs]

        def block(j, carry, zone):
            c, acc = carry
            kb = k_ref[pl.ds(pl.multiple_of(j * tk, tk), tk), hs]
            vb = v_ref[pl.ds(pl.multiple_of(j * tk, tk), tk), :]
            z = _dot_nt(q, kb)
            lb = jnp.minimum(z, 0.0) - jnp.log(1.0 + jnp.exp(-jnp.abs(z)))
            lom = lb - z
            if zone is not None:
                strict = (col + zone * tk) < row
                lom = jnp.where(strict, lom, 0.0)
            hi = lom.astype(BF16)
            lo = (lom - hi.astype(F32)).astype(BF16)
            r = _dot(hi, uo) + _dot(lo, uo)
            w = jnp.exp(lb + r[:, :tk] + c)
            if zone is not None:
                w = jnp.where(strict, w, 0.0)
            acc = acc + _dot(w.astype(BF16), vb)
            return c + r[:, tk:], acc

        carry = (jnp.zeros((tq, tk), F32), jnp.zeros((tq, SLOT), F32))
        for zi in reversed(range(nz)):
            carry = block(qi * nz + zi, carry, zi)
        n_full = qi * nz
        carry = lax.fori_loop(0, n_full, lambda jj, cr: block(n_full - 1 - jj, cr, None), carry)
        outs.append(carry[1])
    lane = lax.broadcasted_iota(jnp.int32, (tq, SLOT), 1)
    o_ref[...] = jnp.where(lane < HEAD_DIM, outs[0], outs[1])


def _sb_attn(ob, batch, s_len):
    n = ob.shape[0]
    tq, tk = TQ_SB, TK_SB
    nq = s_len // tq
    qc, kc, vc = OB_SQ // (2 * SLOT), OB_SK // (2 * SLOT), OB_SV // SLOT
    return pl.pallas_call(
        functools.partial(_sb_attn_kernel, tq=tq, tk=tk),
        grid=(batch, N_HEADS // 2, nq),
        in_specs=[
            pl.BlockSpec((tq, 2 * SLOT), lambda b, p, i: (b * nq + i, qc + p)),
            pl.BlockSpec((s_len, 2 * SLOT), lambda b, p, i: (b, kc + p)),
            pl.BlockSpec((s_len, SLOT), lambda b, p, i: (b, vc + p)),
        ],
        out_specs=pl.BlockSpec((tq, SLOT), lambda b, p, i: (b * nq + i, p)),
        out_shape=jax.ShapeDtypeStruct((n, (N_HEADS // 2) * SLOT), F32),
        compiler_params=_cparams(("parallel", "parallel", "arbitrary")),
        name="sb_attn",
    )(ob, ob, ob)


def _mla_proj_kernel(cq_ref, ckv_ref, kra_ref, krb_ref, pos_ref, invf_ref, gq_ref, gkv_ref,
                     wqa_ref, wqb_ref, wk_ref, wv_ref, q_ref, k_ref, v_ref, *, scale):
    ang = pos_ref[...].astype(F32) * invf_ref[...]
    cs, sn = jnp.cos(ang), jnp.sin(ang)
    cqn = _rms(cq_ref[...], gq_ref[...]).astype(BF16)
    ckvn = _rms(ckv_ref[...], gkv_ref[...]).astype(BF16)
    qa, qb = _dot(cqn, wqa_ref[...]), _dot(cqn, wqb_ref[...])
    kn = _dot(ckvn, wk_ref[...])
    v_ref[...] = _dot(ckvn, wv_ref[...]).astype(BF16)
    kr = kra_ref[...] * cs + krb_ref[...] * sn
    for h in range(N_HEADS):
        hs = slice(h * SLOT, (h + 1) * SLOT)
        q_ref[:, hs] = ((qa[:, hs] * cs + qb[:, hs] * sn) * scale).astype(BF16)
        k_ref[:, hs] = (kn[:, hs] + kr).astype(BF16)


def _mla_proj(of, pos, invf, gq, gkv, wqa, wqb, wk, wv):
    n = of.shape[0]
    tm = TM_PROJ
    w = N_HEADS * SLOT
    const = lambda i: (0, 0)
    return pl.pallas_call(
        functools.partial(_mla_proj_kernel, scale=(MLA_NOPE + MLA_ROPE) ** -0.5),
        grid=(n // tm,),
        in_specs=[
            pl.BlockSpec((tm, MLA_Q_RANK), lambda i: (i, OF_CQ // MLA_Q_RANK)),
            pl.BlockSpec((tm, LANES), lambda i: (i, OF_CKV // LANES)),
            pl.BlockSpec((tm, LANES), lambda i: (i, OF_KRA // LANES)),
            pl.BlockSpec((tm, LANES), lambda i: (i, OF_KRB // LANES)),
            pl.BlockSpec((tm, 1), lambda i: (i, 0)),
            pl.BlockSpec((1, LANES), const),
            pl.BlockSpec((1, MLA_Q_RANK), const),
            pl.BlockSpec((1, MLA_KV_RANK), const),
            pl.BlockSpec((MLA_Q_RANK, w), const),
            pl.BlockSpec((MLA_Q_RANK, w), const),
            pl.BlockSpec((MLA_KV_RANK, w), const),
            pl.BlockSpec((MLA_KV_RANK, w), const),
        ],
        out_specs=[pl.BlockSpec((tm, w), lambda i: (i, 0))] * 3,
        out_shape=[jax.ShapeDtypeStruct((n, w), BF16)] * 3,
        compiler_params=_cparams(("parallel",)),
        name="mla_proj",
    )(of, of, of, of, pos, invf, gq, gkv, wqa, wqb, wk, wv)


def _outproj_kernel(yf_ref, ys_ref, ym_ref, g_ref, wo_ref, h_ref, o_ref):
    gw = yf_ref.shape[1]
    nf = _rms(yf_ref[...], g_ref[:, 0:gw]).astype(BF16)
    ns = _rms(ys_ref[...], g_ref[:, gw:2 * gw]).astype(BF16)
    nm = _rms(ym_ref[...], g_ref[:, 2 * gw:]).astype(BF16)
    acc = _dot(nf, wo_ref[0:gw, :]) + _dot(ns, wo_ref[gw:2 * gw, :]) + _dot(nm, wo_ref[2 * gw:, :])
    o_ref[...] = h_ref[...] + acc


def _outproj(yf, ys, ym, g, wo, h):
    n = h.shape[0]
    tm = TM_PROJ
    const = lambda i: (0, 0)
    return pl.pallas_call(
        _outproj_kernel,
        grid=(n // tm,),
        in_specs=[
            pl.BlockSpec((tm, yf.shape[1]), lambda i: (i, 0)),
            pl.BlockSpec((tm, ys.shape[1]), lambda i: (i, 0)),
            pl.BlockSpec((tm, ym.shape[1]), lambda i: (i, 0)),
            pl.BlockSpec((1, D_MODEL), const),
            pl.BlockSpec((D_MODEL, D_MODEL), const),
            pl.BlockSpec((tm, D_MODEL), lambda i: (i, 0)),
        ],
        out_specs=pl.BlockSpec((tm, D_MODEL), lambda i: (i, 0)),
        out_shape=jax.ShapeDtypeStruct((n, D_MODEL), F32),
        compiler_params=_cparams(("parallel",)),
        name="outproj",
    )(yf, ys, ym, g, wo, h)


def _top_rows(e, n):
    rows = []
    for r in range(n):
        m = jnp.max(e, axis=0, keepdims=True)
        rows.append(m)
        if r + 1 < n:
            e = jnp.where(e == m, -1.0, e)
    return rows


def _peer_kernel(h_ref, g_ref, wq_ref, sk_ref, u_ref, vt_ref, o_ref,
                 xnt_ref, e2_ref, e1s_ref, g16_ref, tv_ref, cand_ref, candn_ref, act_ref, acc_ref, *, tile, eb):
    s = pl.program_id(1)

    @pl.when(s == 0)
    def _():
        xn = _rms(h_ref[...], g_ref[...])
        xnt = xn.T
        xnt_ref[...] = xnt.astype(BF16)
        qt = _dot_hi(wq_ref[...], xnt)
        cand_ref[N_CAND:, :] = jnp.full((N_CAND_PAD - N_CAND, tile), -1.0, F32)
        candn_ref[N_CAND:, :] = jnp.full((N_CAND_PAD - N_CAND, tile), -1.0, F32)
        for hd in range(PEER_HEADS):
            es = []
            for p in range(2):
                base = (hd * 2 + p) * PEER_HALF
                sc = _dot_hi(sk_ref[hd * 2 + p], qt[base:base + PEER_HALF, :])
                e = jnp.exp(sc - jnp.max(sc, axis=0, keepdims=True))
                es.append(e)
                for r, m in enumerate(_top_rows(e, PEER_TOPK)):
                    tv_ref[p, r:r + 1, :] = m
            for i, (a, b) in enumerate(PEER_PAIRS):
                cand_ref[i:i + 1, :] = tv_ref[0, a:a + 1, :] * tv_ref[1, b:b + 1, :]
            cand = cand_ref[...]
            tau = _top_rows(cand, PEER_TOPK)[-1]
            sel = cand >= tau
            inv_z = 1.0 / jnp.sum(jnp.where(sel, cand, 0.0), axis=0, keepdims=True)
            e1s_ref[hd] = es[0] * inv_z
            e2_ref[hd] = es[1]
            t1s = tv_ref[0] * inv_z
            for i, (a, b) in enumerate(PEER_PAIRS):
                candn_ref[i:i + 1, :] = t1s[a:a + 1, :] * tv_ref[1, b:b + 1, :]
            g16_ref[hd] = jnp.min(jnp.where(sel, candn_ref[...], jnp.inf), axis=0, keepdims=True)
        acc_ref[...] = jnp.zeros_like(acc_ref)

    pre = _dot(u_ref[...], xnt_ref[...])
    for a in range(eb // PEER_NKEYS):
        n1 = s * (eb // PEER_NKEYS) + a
        gate = None
        for hd in range(PEER_HEADS):
            val = e2_ref[hd] * e1s_ref[hd, pl.ds(n1, 1), :]
            contrib = jnp.where(val >= g16_ref[hd], val, 0.0)
            gate = contrib if gate is None else gate + contrib
        pa = pre[a * PEER_NKEYS:(a + 1) * PEER_NKEYS, :]
        gelu = 0.5 * pa * (1.0 + lax.erf(pa * (1.0 / math.sqrt(2.0))))
        act_ref[a * PEER_NKEYS:(a + 1) * PEER_NKEYS, :] = (gate * gelu).astype(BF16)
    acc_ref[...] += _dot(vt_ref[...], act_ref[...])

    @pl.when(s == pl.num_programs(1) - 1)
    def _():
        o_ref[...] = h_ref[...] + acc_ref[...].T


def _peer(h, g, wqt, sk, u, vt):
    n = h.shape[0]
    n_exp = u.shape[0]
    tile, eb = T_PEER, EB_PEER
    dq = wqt.shape[0]
    return pl.pallas_call(
        functools.partial(_peer_kernel, tile=tile, eb=eb),
        grid=(n // tile, n_exp // eb),
        in_specs=[
            pl.BlockSpec((tile, D_MODEL), lambda i, s: (i, 0)),
            pl.BlockSpec((1, D_MODEL), lambda i, s: (0, 0)),
            pl.BlockSpec((dq, D_MODEL), lambda i, s: (0, 0)),
            pl.BlockSpec((2 * PEER_HEADS, PEER_NKEYS, PEER_HALF), lambda i, s: (0, 0, 0)),
            pl.BlockSpec((eb, D_MODEL), lambda i, s: (s, 0)),
            pl.BlockSpec((D_MODEL, eb), lambda i, s: (0, s)),
        ],
        out_specs=pl.BlockSpec((tile, D_MODEL), lambda i, s: (i, 0)),
        out_shape=jax.ShapeDtypeStruct((n, D_MODEL), F32),
        scratch_shapes=[
            pltpu.VMEM((D_MODEL, tile), BF16),
            pltpu.VMEM((PEER_HEADS, PEER_NKEYS, tile), F32),
            pltpu.VMEM((PEER_HEADS, PEER_NKEYS, tile), F32),
            pltpu.VMEM((PEER_HEADS, 1, tile), F32),
            pltpu.VMEM((2, PEER_TOPK, tile), F32),
            pltpu.VMEM((N_CAND_PAD, tile), F32),
            pltpu.VMEM((N_CAND_PAD, tile), F32),
            pltpu.VMEM((eb, tile), BF16),
            pltpu.VMEM((D_MODEL, tile), F32),
        ],
        compiler_params=_cparams(("parallel", "arbitrary")),
        name="peer",
    )(h, g, wqt, sk, u, vt)


def _ple_kernel(h_ref, p_ref, g_ref, wg_ref, wp_ref, fg_ref, o_ref, *, final):
    h = h_ref[...]
    gate = jax.nn.sigmoid(_dot(_rms(h, g_ref[...]).astype(BF16), wg_ref[...]))
    out = h + gate * _dot(p_ref[...].astype(BF16), wp_ref[...])
    o_ref[...] = _rms(out, fg_ref[...]) if final else out


def _ple(h, p, g, wg, wp, fg, final):
    n = h.shape[0]
    tm = TM_PROJ
    const = lambda i: (0, 0)
    return pl.pallas_call(
        functools.partial(_ple_kernel, final=final),
        grid=(n // tm,),
        in_specs=[
            pl.BlockSpec((tm, D_MODEL), lambda i: (i, 0)),
            pl.BlockSpec((tm, PLE_DIM), lambda i: (i, 0)),
            pl.BlockSpec((1, D_MODEL), const),
            pl.BlockSpec((D_MODEL, D_MODEL), const),
            pl.BlockSpec((PLE_DIM, D_MODEL), const),
            pl.BlockSpec((1, D_MODEL), const),
        ],
        out_specs=pl.BlockSpec((tm, D_MODEL), lambda i: (i, 0)),
        out_shape=jax.ShapeDtypeStruct((n, D_MODEL), F32),
        compiler_params=_cparams(("parallel",)),
        name="ple",
    )(h, p, g, wg, wp, fg)


def _pad_heads(w, width=SLOT):
    k = w.shape[0]
    w = w.reshape(k, N_HEADS, -1)
    return jnp.pad(w, ((0, 0), (0, 0), (0, width - w.shape[2]))).reshape(k, N_HEADS * width)


def _rot_partner(w_rot):
    half = MLA_ROPE // 2
    return jnp.concatenate([-w_rot[..., half:], w_rot[..., :half]], axis=-1)


def _layer_weights(w_in, w_uq, w_ukv):
    offs = [int(o) for o in np.cumsum(IN_SPLITS)[:-1]]
    fq, fk, fv, fl, sq, sk, sv, cq, ckv, kr = jnp.split(w_in, offs, axis=1)
    att_scale = HEAD_DIM ** -0.5
    wb = jnp.concatenate([_pad_heads(fq * att_scale), _pad_heads(fk), fv,
                          _pad_heads(sq * att_scale), _pad_heads(sk), sv], axis=1).astype(BF16)
    z = lambda c: jnp.zeros((D_MODEL, c), F32)
    kra = jnp.concatenate([z(MLA_NOPE), kr, z(SLOT - MLA_NOPE - MLA_ROPE)], axis=1)
    krb = jnp.concatenate([z(MLA_NOPE), _rot_partner(kr), z(SLOT - MLA_NOPE - MLA_ROPE)], axis=1)
    wf = jnp.concatenate([cq, ckv, kra, krb, fl, z(LANES - N_HEADS)], axis=1).astype(BF16)
    uq = w_uq.reshape(MLA_Q_RANK, N_HEADS, MLA_NOPE + MLA_ROPE)
    zq = jnp.zeros((MLA_Q_RANK, N_HEADS, SLOT - MLA_NOPE - MLA_ROPE), F32)
    wqa = jnp.concatenate([uq, zq], axis=2).reshape(MLA_Q_RANK, N_HEADS * SLOT).astype(BF16)
    wqb = jnp.concatenate([jnp.zeros_like(uq[..., :MLA_NOPE]), _rot_partner(uq[..., MLA_NOPE:]), zq],
                          axis=2).reshape(MLA_Q_RANK, N_HEADS * SLOT).astype(BF16)
    ukv = w_ukv.reshape(MLA_KV_RANK, N_HEADS, MLA_NOPE + MLA_V)
    wk = jnp.pad(ukv[..., :MLA_NOPE], ((0, 0), (0, 0), (0, SLOT - MLA_NOPE))).reshape(MLA_KV_RANK, -1).astype(BF16)
    wv = ukv[..., MLA_NOPE:].reshape(MLA_KV_RANK, -1).astype(BF16)
    return wb, wf, wqa, wqb, wk, wv


def _rope_lane_freqs():
    half = MLA_ROPE // 2
    inv_freq = ROPE_THETA ** (-jnp.arange(half, dtype=F32) / half)
    zeros = lambda c: jnp.zeros((c,), F32)
    return jnp.concatenate([zeros(MLA_NOPE), inv_freq, inv_freq, zeros(SLOT - MLA_NOPE - MLA_ROPE)]).reshape(1, SLOT)


def kernel(x, p, positions, norm_mix_g, w_in, b_forget, mla_q_norm_g, w_uq, mla_kv_norm_g, w_ukv, mix_out_norm_g, w_o, norm_ffn_g, peer_w_query, peer_sub_keys, peer_u, peer_v, w_ple, ple_norm_g, w_ple_gate, final_norm_g):
    batch, s_len, d = x.shape
    depth = p.shape[0]
    n = batch * s_len
    h = x.reshape(n, d)
    pos = positions.reshape(n, 1)
    invf = _rope_lane_freqs()
    row = lambda v: v.reshape(1, -1)
    for i in range(depth):
        wb, wf, wqa, wqb, wk, wv = _layer_weights(w_in[i], w_uq[i], w_ukv[i])
        ob, of = _inproj(h, row(norm_mix_g[i]), wb, wf)
        bias = jnp.pad(b_forget[i], (0, LANES - N_HEADS)).reshape(1, LANES)
        fqk = _foxprep(ob, of, bias, batch, s_len)
        y_fox = _softmax_attn(fqk, 0, fqk, N_HEADS // 2, ob, OB_FV // SLOT, batch, s_len, True)
        y_sb = _sb_attn(ob, batch, s_len)
        mq, mk, mv = _mla_proj(of, pos, invf, row(mla_q_norm_g[i]), row(mla_kv_norm_g[i]), wqa, wqb, wk, wv)
        y_mla = _softmax_attn(mq, 0, mk, 0, mv, 0, batch, s_len, False)
        h = _outproj(y_fox, y_sb, y_mla, row(mix_out_norm_g[i]), w_o[i].astype(BF16), h)
        sk = peer_sub_keys[i].reshape(2 * PEER_HEADS, PEER_NKEYS, PEER_HALF)
        h = _peer(h, row(norm_ffn_g[i]), peer_w_query[i].T, sk, peer_u[i].astype(BF16), peer_v[i].T.astype(BF16))
        h = _ple(h, p[i].reshape(n, PLE_DIM), row(ple_norm_g[i]), w_ple_gate[i].astype(BF16),
                 w_ple[i].astype(BF16), row(final_norm_g), i == depth - 1)
    return h.reshape(batch, s_len, d)
```

```python
import functools
import math

import numpy as np
import jax
import jax.numpy as jnp
from jax import lax
from jax.experimental import pallas as pl
from jax.experimental.pallas import tpu as pltpu

F32 = jnp.float32
BF16 = jnp.bfloat16

D_MODEL = 1024
HEAD_DIM = 64
N_HEADS = 4
MLA_Q_RANK = 256
MLA_KV_RANK = 128
MLA_NOPE = 64
MLA_ROPE = 32
MLA_V = 128
ROPE_THETA = 10000.0
IN_SPLITS = (256, 256, 256, 4, 256, 256, 256, MLA_Q_RANK, MLA_KV_RANK, MLA_ROPE)
PEER_HEADS = 8
PEER_NKEYS = 128
PEER_HALF = 64
PEER_TOPK = 16
PLE_DIM = 256
EPS = 1e-6

LANES = 128
SLOT = LANES
VMEM_LIMIT = 56 * 1024 * 1024

TM_PROJ = 512
TQ_SOFTMAX = 256
TQ_SB = 256
TK_SB = 128
T_PEER = 512
EB_PEER = 512
MXU_PIECES = 4

OB_FQ, OB_FK, OB_SQ, OB_SK, OB_FV, OB_SV, OB_W = 0, 512, 1024, 1536, 2048, 2304, 2560
GW = N_HEADS * SLOT
OF_CQ, OF_CKV, OF_KRA, OF_KRB, OF_FL, OF_W = 0, 256, 384, 512, 640, 768

PEER_PAIRS = tuple((a, b) for a in range(PEER_TOPK) for b in range(PEER_TOPK) if (a + 1) * (b + 1) <= PEER_TOPK)
N_CAND = len(PEER_PAIRS)
N_CAND_PAD = 56


def _cparams(sem):
    return pltpu.CompilerParams(dimension_semantics=sem, vmem_limit_bytes=VMEM_LIMIT)


def _rms(x, g):
    return x * lax.rsqrt(jnp.mean(x * x, axis=-1, keepdims=True) + EPS) * g


def _dot(a, b):
    return jnp.dot(a, b, preferred_element_type=F32)


def _dot_nt(a, b):
    return lax.dot_general(a, b, (((1,), (1,)), ((), ())), preferred_element_type=F32)


def _dot_hi(a, b):
    return jnp.dot(a, b, preferred_element_type=F32, precision=lax.Precision.HIGHEST)


def _inproj_kernel(x_ref, g_ref, wb_ref, wf_ref, ob_ref, of_ref):
    xb = _rms(x_ref[...], g_ref[...]).astype(BF16)
    ob_ref[...] = _dot(xb, wb_ref[...]).astype(BF16)
    of_ref[...] = _dot(xb, wf_ref[...])


def _inproj(h, g, wb, wf):
    n = h.shape[0]
    tm = TM_PROJ
    return pl.pallas_call(
        _inproj_kernel,
        grid=(n // tm,),
        in_specs=[
            pl.BlockSpec((tm, D_MODEL), lambda i: (i, 0)),
            pl.BlockSpec((1, D_MODEL), lambda i: (0, 0)),
            pl.BlockSpec((D_MODEL, OB_W), lambda i: (0, 0)),
            pl.BlockSpec((D_MODEL, OF_W), lambda i: (0, 0)),
        ],
        out_specs=[
            pl.BlockSpec((tm, OB_W), lambda i: (i, 0)),
            pl.BlockSpec((tm, OF_W), lambda i: (i, 0)),
        ],
        out_shape=[jax.ShapeDtypeStruct((n, OB_W), BF16), jax.ShapeDtypeStruct((n, OF_W), F32)],
        compiler_params=_cparams(("parallel",)),
        name="inproj",
    )(h, g, wb, wf)


def _split3(x):
    hi = x.astype(BF16)
    r = x - hi.astype(F32)
    mid = r.astype(BF16)
    lo = (r - mid.astype(F32)).astype(BF16)
    return hi, mid, lo


def _foxprep_kernel(qk_ref, fl_ref, b_ref, o_ref, f_ref, *, rb):
    s_len = fl_ref.shape[0]
    x = fl_ref[...] + b_ref[...]
    logf = jnp.minimum(x, 0.0) - jnp.log(1.0 + jnp.exp(-jnp.abs(x)))
    hi, mid, lo = _split3(logf)
    for r in range(s_len // rb):
        row = lax.broadcasted_iota(jnp.int32, (rb, s_len), 0) + r * rb
        col = lax.broadcasted_iota(jnp.int32, (rb, s_len), 1)
        tri = jnp.where(col <= row, 1.0, 0.0).astype(BF16)
        f_ref[r * rb:(r + 1) * rb, :] = _dot(tri, hi) + _dot(tri, mid) + _dot(tri, lo)
    f = f_ref[...]
    lane = lax.broadcasted_iota(jnp.int32, (s_len, LANES), 1)
    for h in range(N_HEADS):
        fh = f[:, h:h + 1]
        fhi, fmid, flo = (t.astype(F32) for t in _split3(fh))
        qa = jnp.where(lane == 64, fhi, jnp.where(lane == 65, fmid, jnp.where(lane == 66, flo,
             jnp.where((lane >= 67) & (lane < 70), 1.0, 0.0))))
        ka = jnp.where(lane == 67, -fhi, jnp.where(lane == 68, -fmid, jnp.where(lane == 69, -flo,
             jnp.where((lane >= 64) & (lane < 67), 1.0, 0.0))))
        qs = slice(h * SLOT, (h + 1) * SLOT)
        ks = slice(N_HEADS * SLOT + h * SLOT, N_HEADS * SLOT + (h + 1) * SLOT)
        o_ref[:, qs] = (qk_ref[:, qs].astype(F32) + qa).astype(BF16)
        o_ref[:, ks] = (qk_ref[:, ks].astype(F32) + ka).astype(BF16)


def _foxprep(ob, of, bias, batch, s_len):
    n = ob.shape[0]
    w = 2 * N_HEADS * SLOT
    return pl.pallas_call(
        functools.partial(_foxprep_kernel, rb=256),
        grid=(batch,),
        in_specs=[
            pl.BlockSpec((s_len, w), lambda b: (b, 0)),
            pl.BlockSpec((s_len, LANES), lambda b: (b, OF_FL // LANES)),
            pl.BlockSpec((1, LANES), lambda b: (0, 0)),
        ],
        out_specs=pl.BlockSpec((s_len, w), lambda b: (b, 0)),
        out_shape=jax.ShapeDtypeStruct((n, w), BF16),
        scratch_shapes=[pltpu.VMEM((s_len, LANES), F32)],
        compiler_params=_cparams(("parallel",)),
        name="foxprep",
    )(ob, of, bias)


def _softmax_attn_kernel(q_ref, k_ref, v_ref, o_ref, *, tq, pair_sum):
    qi = pl.program_id(1)
    row = lax.broadcasted_iota(jnp.int32, (tq, tq), 0)
    col = lax.broadcasted_iota(jnp.int32, (tq, tq), 1)
    causal = col <= row
    slots = [slice(h * SLOT, (h + 1) * SLOT) for h in range(N_HEADS)]
    qs = [q_ref[:, hs] for hs in slots]

    ones = jnp.ones((tq, SLOT), BF16)

    def block(j, carry, masked):
        rows = pl.ds(pl.multiple_of(j * tq, tq), tq)
        ss = [_dot_nt(qs[h], k_ref[rows, slots[h]]) for h in range(N_HEADS)]
        ms, alphas, ps = [], [], []
        for h in range(N_HEADS):
            s = jnp.where(causal, ss[h], -1e30) if masked else ss[h]
            m_new = jnp.maximum(carry[h][0], jnp.max(s, axis=1, keepdims=True))
            ms.append(m_new)
            alphas.append(jnp.exp(carry[h][0] - m_new))
            ps.append(jnp.exp(s - m_new).astype(BF16))
        if pair_sum:
            pvs = []
            for pr in range(N_HEADS // 2):
                vx = jnp.concatenate([v_ref[rows, slots[pr]], ones], axis=1)
                both = _dot(jnp.concatenate([ps[2 * pr], ps[2 * pr + 1]], axis=0), vx)
                pvs += [both[:tq], both[tq:]]
        else:
            pvs = [_dot(ps[h], jnp.concatenate([v_ref[rows, slots[h]], ones], axis=1)) for h in range(N_HEADS)]
        return tuple((ms[h], alphas[h] * carry[h][1] + pvs[h][:, SLOT:], alphas[h] * carry[h][2] + pvs[h][:, :SLOT])
                     for h in range(N_HEADS))

    init = tuple((jnp.full((tq, 1), -1e30, F32), jnp.zeros((tq, SLOT), F32), jnp.zeros((tq, SLOT), F32))
                 for _ in range(N_HEADS))
    carry = lax.fori_loop(0, qi, lambda j, c: block(j, c, False), init)
    outs = [acc / l for _, l, acc in block(qi, carry, True)]
    if pair_sum:
        lane = lax.broadcasted_iota(jnp.int32, (tq, SLOT), 1)
        for pr in range(N_HEADS // 2):
            o_ref[:, slots[pr]] = jnp.where(lane < HEAD_DIM, outs[2 * pr], outs[2 * pr + 1])
    else:
        for h in range(N_HEADS):
            o_ref[:, slots[h]] = outs[h]


def _softmax_attn(q_arr, q_col, k_arr, k_col, v_arr, v_col, batch, s_len, pair_sum):
    n = q_arr.shape[0]
    tq = TQ_SOFTMAX
    nq = s_len // tq
    vw = GW // 2 if pair_sum else GW
    return pl.pallas_call(
        functools.partial(_softmax_attn_kernel, tq=tq, pair_sum=pair_sum),
        grid=(batch, nq),
        in_specs=[
            pl.BlockSpec((tq, GW), lambda b, i: (b * nq + i, q_col)),
            pl.BlockSpec((s_len, GW), lambda b, i: (b, k_col)),
            pl.BlockSpec((s_len, vw), lambda b, i: (b, v_col)),
        ],
        out_specs=pl.BlockSpec((tq, vw), lambda b, i: (b * nq + i, 0)),
        out_shape=jax.ShapeDtypeStruct((n, vw), F32),
        compiler_params=_cparams(("parallel", "arbitrary")),
        name="fox_attn" if pair_sum else "mla_attn",
    )(q_arr, k_arr, v_arr)


def _sb_attn_kernel(q_ref, k_ref, v_ref, o_ref, *, tq, tk):
    qi = pl.program_id(1)
    nz = tq // tk
    row = lax.broadcasted_iota(jnp.int32, (tq, tk), 0)
    col = lax.broadcasted_iota(jnp.int32, (tq, tk), 1)
    ur = lax.broadcasted_iota(jnp.int32, (tk, 2 * tk), 0)
    uc = lax.broadcasted_iota(jnp.int32, (tk, 2 * tk), 1)
    uo = jnp.where((uc >= tk) | (ur > uc), 1.0, 0.0).astype(BF16)
    slots = [slice(h * SLOT, (h + 1) * SLOT) for h in range(N_HEADS)]
    qs = [q_ref[:, hs] for hs in slots]

    lane = lax.broadcasted_iota(jnp.int32, (tq, SLOT), 1)

    def block(j, carry, zone):
        cs, accs = carry
        rows = pl.ds(pl.multiple_of(j * tk, tk), tk)
        zs = [_dot_nt(qs[h], k_ref[rows, slots[h]]) for h in range(N_HEADS)]
        strict = None if zone is None else (col + zone * tk) < row
        lbs, his, los = [], [], []
        for z in zs:
            lb = jnp.minimum(z, 0.0) - jnp.log(1.0 + jnp.exp(-jnp.abs(z)))
            lom = lb - z
            if strict is not None:
                lom = jnp.where(strict, lom, 0.0)
            hi = lom.astype(BF16)
            lbs.append(lb)
            his.append(hi)
            los.append((lom - hi.astype(F32)).astype(BF16))
        r = _dot(jnp.concatenate(his + los, axis=0), uo)
        ws, new_cs = [], []
        for h in range(N_HEADS):
            rh = r[h * tq:(h + 1) * tq] + r[(N_HEADS + h) * tq:(N_HEADS + h + 1) * tq]
            w = jnp.exp(lbs[h] + rh[:, :tk] + cs[h])
            if strict is not None:
                w = jnp.where(strict, w, 0.0)
            ws.append(w.astype(BF16))
            new_cs.append(cs[h] + rh[:, tk:])
        new_accs = []
        for pr in range(N_HEADS // 2):
            both = _dot(jnp.concatenate([ws[2 * pr], ws[2 * pr + 1]], axis=0), v_ref[rows, slots[pr]])
            new_accs.append(accs[pr] + jnp.where(lane < HEAD_DIM, both[:tq], both[tq:]))
        return tuple(new_cs), tuple(new_accs)

    carry = (tuple(jnp.zeros((tq, tk), F32) for _ in range(N_HEADS)),
             tuple(jnp.zeros((tq, SLOT), F32) for _ in range(N_HEADS // 2)))
    for zi in reversed(range(nz)):
        carry = block(qi * nz + zi, carry, zi)
    n_full = qi * nz
    carry = lax.fori_loop(0, n_full, lambda jj, cr: block(n_full - 1 - jj, cr, None), carry)
    for pr in range(N_HEADS // 2):
        o_ref[:, slots[pr]] = carry[1][pr]


def _sb_attn(ob, batch, s_len):
    n = ob.shape[0]
    tq, tk = TQ_SB, TK_SB
    nq = s_len // tq
    vw = GW // 2
    return pl.pallas_call(
        functools.partial(_sb_attn_kernel, tq=tq, tk=tk),
        grid=(batch, nq),
        in_specs=[
            pl.BlockSpec((tq, GW), lambda b, i: (b * nq + i, OB_SQ // GW)),
            pl.BlockSpec((s_len, GW), lambda b, i: (b, OB_SK // GW)),
            pl.BlockSpec((s_len, vw), lambda b, i: (b, OB_SV // vw)),
        ],
        out_specs=pl.BlockSpec((tq, vw), lambda b, i: (b * nq + i, 0)),
        out_shape=jax.ShapeDtypeStruct((n, vw), F32),
        compiler_params=_cparams(("parallel", "arbitrary")),
        name="sb_attn",
    )(ob, ob, ob)


def _mla_proj_kernel(cq_ref, ckv_ref, kra_ref, krb_ref, pos_ref, invf_ref, gq_ref, gkv_ref,
                     wqa_ref, wqb_ref, wk_ref, wv_ref, q_ref, k_ref, v_ref, *, scale):
    ang = pos_ref[...].astype(F32) * invf_ref[...]
    cs, sn = jnp.cos(ang), jnp.sin(ang)
    cqn = _rms(cq_ref[...], gq_ref[...]).astype(BF16)
    ckvn = _rms(ckv_ref[...], gkv_ref[...]).astype(BF16)
    qa, qb = _dot(cqn, wqa_ref[...]), _dot(cqn, wqb_ref[...])
    kn = _dot(ckvn, wk_ref[...])
    v_ref[...] = _dot(ckvn, wv_ref[...]).astype(BF16)
    kr = kra_ref[...] * cs + krb_ref[...] * sn
    for h in range(N_HEADS):
        hs = slice(h * SLOT, (h + 1) * SLOT)
        q_ref[:, hs] = ((qa[:, hs] * cs + qb[:, hs] * sn) * scale).astype(BF16)
        k_ref[:, hs] = (kn[:, hs] + kr).astype(BF16)


def _mla_proj(of, pos, invf, gq, gkv, wqa, wqb, wk, wv):
    n = of.shape[0]
    tm = TM_PROJ
    w = N_HEADS * SLOT
    const = lambda i: (0, 0)
    return pl.pallas_call(
        functools.partial(_mla_proj_kernel, scale=(MLA_NOPE + MLA_ROPE) ** -0.5),
        grid=(n // tm,),
        in_specs=[
            pl.BlockSpec((tm, MLA_Q_RANK), lambda i: (i, OF_CQ // MLA_Q_RANK)),
            pl.BlockSpec((tm, LANES), lambda i: (i, OF_CKV // LANES)),
            pl.BlockSpec((tm, LANES), lambda i: (i, OF_KRA // LANES)),
            pl.BlockSpec((tm, LANES), lambda i: (i, OF_KRB // LANES)),
            pl.BlockSpec((tm, 1), lambda i: (i, 0)),
            pl.BlockSpec((1, LANES), const),
            pl.BlockSpec((1, MLA_Q_RANK), const),
            pl.BlockSpec((1, MLA_KV_RANK), const),
            pl.BlockSpec((MLA_Q_RANK, w), const),
            pl.BlockSpec((MLA_Q_RANK, w), const),
            pl.BlockSpec((MLA_KV_RANK, w), const),
            pl.BlockSpec((MLA_KV_RANK, w), const),
        ],
        out_specs=[pl.BlockSpec((tm, w), lambda i: (i, 0))] * 3,
        out_shape=[jax.ShapeDtypeStruct((n, w), BF16)] * 3,
        compiler_params=_cparams(("parallel",)),
        name="mla_proj",
    )(of, of, of, of, pos, invf, gq, gkv, wqa, wqb, wk, wv)


def _outproj_kernel(yf_ref, ys_ref, ym_ref, g_ref, wo_ref, h_ref, o_ref):
    gw = yf_ref.shape[1]
    nf = _rms(yf_ref[...], g_ref[:, 0:gw]).astype(BF16)
    ns = _rms(ys_ref[...], g_ref[:, gw:2 * gw]).astype(BF16)
    nm = _rms(ym_ref[...], g_ref[:, 2 * gw:]).astype(BF16)
    acc = _dot(nf, wo_ref[0:gw, :]) + _dot(ns, wo_ref[gw:2 * gw, :]) + _dot(nm, wo_ref[2 * gw:, :])
    o_ref[...] = h_ref[...] + acc


def _outproj(yf, ys, ym, g, wo, h):
    n = h.shape[0]
    tm = TM_PROJ
    const = lambda i: (0, 0)
    return pl.pallas_call(
        _outproj_kernel,
        grid=(n // tm,),
        in_specs=[
            pl.BlockSpec((tm, yf.shape[1]), lambda i: (i, 0)),
            pl.BlockSpec((tm, ys.shape[1]), lambda i: (i, 0)),
            pl.BlockSpec((tm, ym.shape[1]), lambda i: (i, 0)),
            pl.BlockSpec((1, D_MODEL), const),
            pl.BlockSpec((D_MODEL, D_MODEL), const),
            pl.BlockSpec((tm, D_MODEL), lambda i: (i, 0)),
        ],
        out_specs=pl.BlockSpec((tm, D_MODEL), lambda i: (i, 0)),
        out_shape=jax.ShapeDtypeStruct((n, D_MODEL), F32),
        compiler_params=_cparams(("parallel",)),
        name="outproj",
    )(yf, ys, ym, g, wo, h)


def _top_rows(e, n):
    rows = []
    for r in range(n):
        m = jnp.max(e, axis=0, keepdims=True)
        rows.append(m)
        if r + 1 < n:
            e = jnp.where(e == m, -1.0, e)
    return rows


def _peer_kernel(h_ref, g_ref, wqh_ref, wql_ref, sk_ref, u_ref, vt_ref, o_ref,
                 xnt_ref, e2_ref, e1s_ref, g16_ref, tv_ref, cand_ref, candn_ref,
                 pre0_ref, pre1_ref, act0_ref, act1_ref, acc_ref, *, tile, eb, ns):
    s = pl.program_id(1)

    @pl.when(s == 0)
    def _():
        xn = _rms(h_ref[...], g_ref[...])
        xnt = xn.T
        x_hi = xnt.astype(BF16)
        x_lo = (xnt - x_hi.astype(F32)).astype(BF16)
        xnt_ref[...] = x_hi
        qt = _dot(wqh_ref[...], x_hi) + (_dot(wqh_ref[...], x_lo) + _dot(wql_ref[...], x_hi))
        cand_ref[N_CAND:, :] = jnp.full((N_CAND_PAD - N_CAND, tile), -1.0, F32)
        candn_ref[N_CAND:, :] = jnp.full((N_CAND_PAD - N_CAND, tile), -1.0, F32)
        for hd in range(PEER_HEADS):
            es = []
            for p in range(2):
                base = (hd * 2 + p) * PEER_HALF
                sc = _dot_hi(sk_ref[hd * 2 + p], qt[base:base + PEER_HALF, :])
                e = jnp.exp(sc - jnp.max(sc, axis=0, keepdims=True))
                es.append(e)
                for r, m in enumerate(_top_rows(e, PEER_TOPK)):
                    tv_ref[p, r:r + 1, :] = m
            for i, (a, b) in enumerate(PEER_PAIRS):
                cand_ref[i:i + 1, :] = tv_ref[0, a:a + 1, :] * tv_ref[1, b:b + 1, :]
            cand = cand_ref[...]
            tau = _top_rows(cand, PEER_TOPK)[-1]
            sel = cand >= tau
            inv_z = 1.0 / jnp.sum(jnp.where(sel, cand, 0.0), axis=0, keepdims=True)
            e1s_ref[hd] = es[0] * inv_z
            e2_ref[hd] = es[1]
            t1s = tv_ref[0] * inv_z
            for i, (a, b) in enumerate(PEER_PAIRS):
                candn_ref[i:i + 1, :] = t1s[a:a + 1, :] * tv_ref[1, b:b + 1, :]
            g16_ref[hd] = jnp.min(jnp.where(sel, candn_ref[...], jnp.inf), axis=0, keepdims=True)
        for ref in (pre0_ref, pre1_ref, act0_ref, act1_ref, acc_ref):
            ref[...] = jnp.zeros_like(ref)

    npc = eb // PEER_NKEYS
    n1_base = jnp.clip(s - 1, 0, ns - 1) * npc
    pre_refs, act_refs = (pre0_ref, pre1_ref), (act0_ref, act1_ref)

    def stage_act(a, prv, lane_groups):
        rc = PEER_NKEYS // 2
        e1full = [e1s_ref[hd, pl.ds(n1_base + a, 1), :] for hd in range(PEER_HEADS)]
        for lg in lane_groups:
            cols = slice(lg * LANES, (lg + 1) * LANES)
            e1rows = [row[:, cols] for row in e1full]
            g16rows = [g16_ref[hd, :, cols] for hd in range(PEER_HEADS)]
            for r0 in range(0, PEER_NKEYS, rc):
                gate = None
                for hd in range(PEER_HEADS):
                    val = e2_ref[hd, r0:r0 + rc, cols] * e1rows[hd]
                    contrib = jnp.where(val >= g16rows[hd], val, 0.0)
                    gate = contrib if gate is None else gate + contrib
                rows = slice(a * PEER_NKEYS + r0, a * PEER_NKEYS + r0 + rc)
                pa = pre_refs[prv][rows, cols]
                gelu = 0.5 * pa * (1.0 + lax.erf(pa * (1.0 / math.sqrt(2.0))))
                act_refs[prv][rows, cols] = (gate * gelu).astype(BF16)

    def step(cur):
        prv = 1 - cur
        pre_rows, out_rows = eb // MXU_PIECES, D_MODEL // MXU_PIECES

        def pre_piece(r):
            rows = slice(r * pre_rows, (r + 1) * pre_rows)
            pre_refs[cur][rows, :] = _dot(u_ref[rows, :], xnt_ref[...])

        def out_piece(r):
            rows = slice(r * out_rows, (r + 1) * out_rows)
            acc_ref[rows, :] += _dot(vt_ref[rows, :], act_refs[cur][...])

        mxu = [functools.partial(fn, r) for r in range(MXU_PIECES) for fn in (pre_piece, out_piece)]
        vpu = [functools.partial(stage_act, a, prv, (lg,)) for a in range(npc) for lg in range(tile // LANES)]
        done = 0
        for k, piece in enumerate(mxu):
            piece()
            upto = (k + 1) * len(vpu) // len(mxu)
            for fn in vpu[done:upto]:
                fn()
            done = upto

    for parity in range(2):
        pl.when(s % 2 == parity)(functools.partial(step, parity))

    @pl.when(s == pl.num_programs(1) - 1)
    def _():
        o_ref[...] = h_ref[...] + acc_ref[...].T


def _peer(h, g, wqt_hi, wqt_lo, sk, u, vt):
    n = h.shape[0]
    n_exp = u.shape[0]
    tile, eb = T_PEER, EB_PEER
    ns = n_exp // eb
    dq = wqt_hi.shape[0]
    u_map = lambda i, s: (jnp.minimum(s, ns - 1), 0)
    vt_map = lambda i, s: (0, jnp.clip(s - 2, 0, ns - 1))
    return pl.pallas_call(
        functools.partial(_peer_kernel, tile=tile, eb=eb, ns=ns),
        grid=(n // tile, ns + 2),
        in_specs=[
            pl.BlockSpec((tile, D_MODEL), lambda i, s: (i, 0)),
            pl.BlockSpec((1, D_MODEL), lambda i, s: (0, 0)),
            pl.BlockSpec((dq, D_MODEL), lambda i, s: (0, 0)),
            pl.BlockSpec((dq, D_MODEL), lambda i, s: (0, 0)),
            pl.BlockSpec((2 * PEER_HEADS, PEER_NKEYS, PEER_HALF), lambda i, s: (0, 0, 0)),
            pl.BlockSpec((eb, D_MODEL), u_map),
            pl.BlockSpec((D_MODEL, eb), vt_map),
        ],
        out_specs=pl.BlockSpec((tile, D_MODEL), lambda i, s: (i, 0)),
        out_shape=jax.ShapeDtypeStruct((n, D_MODEL), F32),
        scratch_shapes=[
            pltpu.VMEM((D_MODEL, tile), BF16),
            pltpu.VMEM((PEER_HEADS, PEER_NKEYS, tile), F32),
            pltpu.VMEM((PEER_HEADS, PEER_NKEYS, tile), F32),
            pltpu.VMEM((PEER_HEADS, 1, tile), F32),
            pltpu.VMEM((2, PEER_TOPK, tile), F32),
            pltpu.VMEM((N_CAND_PAD, tile), F32),
            pltpu.VMEM((N_CAND_PAD, tile), F32),
            pltpu.VMEM((eb, tile), F32),
            pltpu.VMEM((eb, tile), F32),
            pltpu.VMEM((eb, tile), BF16),
            pltpu.VMEM((eb, tile), BF16),
            pltpu.VMEM((D_MODEL, tile), F32),
        ],
        compiler_params=_cparams(("parallel", "arbitrary")),
        name="peer",
    )(h, g, wqt_hi, wqt_lo, sk, u, vt)


def _ple_kernel(h_ref, p_ref, g_ref, wg_ref, wp_ref, fg_ref, o_ref, *, final):
    h = h_ref[...]
    gate = jax.nn.sigmoid(_dot(_rms(h, g_ref[...]).astype(BF16), wg_ref[...]))
    out = h + gate * _dot(p_ref[...].astype(BF16), wp_ref[...])
    o_ref[...] = _rms(out, fg_ref[...]) if final else out


def _ple(h, p, g, wg, wp, fg, final):
    n = h.shape[0]
    tm = TM_PROJ
    const = lambda i: (0, 0)
    return pl.pallas_call(
        functools.partial(_ple_kernel, final=final),
        grid=(n // tm,),
        in_specs=[
            pl.BlockSpec((tm, D_MODEL), lambda i: (i, 0)),
            pl.BlockSpec((tm, PLE_DIM), lambda i: (i, 0)),
            pl.BlockSpec((1, D_MODEL), const),
            pl.BlockSpec((D_MODEL, D_MODEL), const),
            pl.BlockSpec((PLE_DIM, D_MODEL), const),
            pl.BlockSpec((1, D_MODEL), const),
        ],
        out_specs=pl.BlockSpec((tm, D_MODEL), lambda i: (i, 0)),
        out_shape=jax.ShapeDtypeStruct((n, D_MODEL), F32),
        compiler_params=_cparams(("parallel",)),
        name="ple",
    )(h, p, g, wg, wp, fg)


def _pad_heads(w, width=SLOT):
    k = w.shape[0]
    w = w.reshape(k, N_HEADS, -1)
    return jnp.pad(w, ((0, 0), (0, 0), (0, width - w.shape[2]))).reshape(k, N_HEADS * width)


def _rot_partner(w_rot):
    half = MLA_ROPE // 2
    return jnp.concatenate([-w_rot[..., half:], w_rot[..., :half]], axis=-1)


def _layer_weights(w_in, w_uq, w_ukv):
    offs = [int(o) for o in np.cumsum(IN_SPLITS)[:-1]]
    fq, fk, fv, fl, sq, sk, sv, cq, ckv, kr = jnp.split(w_in, offs, axis=1)
    att_scale = HEAD_DIM ** -0.5
    wb = jnp.concatenate([_pad_heads(fq * att_scale), _pad_heads(fk),
                          _pad_heads(sq * att_scale), _pad_heads(sk), fv, sv], axis=1).astype(BF16)
    z = lambda c: jnp.zeros((D_MODEL, c), F32)
    kra = jnp.concatenate([z(MLA_NOPE), kr, z(SLOT - MLA_NOPE - MLA_ROPE)], axis=1)
    krb = jnp.concatenate([z(MLA_NOPE), _rot_partner(kr), z(SLOT - MLA_NOPE - MLA_ROPE)], axis=1)
    wf = jnp.concatenate([cq, ckv, kra, krb, fl, z(LANES - N_HEADS)], axis=1).astype(BF16)
    uq = w_uq.reshape(MLA_Q_RANK, N_HEADS, MLA_NOPE + MLA_ROPE)
    zq = jnp.zeros((MLA_Q_RANK, N_HEADS, SLOT - MLA_NOPE - MLA_ROPE), F32)
    wqa = jnp.concatenate([uq, zq], axis=2).reshape(MLA_Q_RANK, N_HEADS * SLOT).astype(BF16)
    wqb = jnp.concatenate([jnp.zeros_like(uq[..., :MLA_NOPE]), _rot_partner(uq[..., MLA_NOPE:]), zq],
                          axis=2).reshape(MLA_Q_RANK, N_HEADS * SLOT).astype(BF16)
    ukv = w_ukv.reshape(MLA_KV_RANK, N_HEADS, MLA_NOPE + MLA_V)
    wk = jnp.pad(ukv[..., :MLA_NOPE], ((0, 0), (0, 0), (0, SLOT - MLA_NOPE))).reshape(MLA_KV_RANK, -1).astype(BF16)
    wv = ukv[..., MLA_NOPE:].reshape(MLA_KV_RANK, -1).astype(BF16)
    return wb, wf, wqa, wqb, wk, wv


def _rope_lane_freqs():
    half = MLA_ROPE // 2
    inv_freq = ROPE_THETA ** (-jnp.arange(half, dtype=F32) / half)
    zeros = lambda c: jnp.zeros((c,), F32)
    return jnp.concatenate([zeros(MLA_NOPE), inv_freq, inv_freq, zeros(SLOT - MLA_NOPE - MLA_ROPE)]).reshape(1, SLOT)


def kernel(x, p, positions, norm_mix_g, w_in, b_forget, mla_q_norm_g, w_uq, mla_kv_norm_g, w_ukv, mix_out_norm_g, w_o, norm_ffn_g, peer_w_query, peer_sub_keys, peer_u, peer_v, w_ple, ple_norm_g, w_ple_gate, final_norm_g):
    batch, s_len, d = x.shape
    depth = p.shape[0]
    n = batch * s_len
    h = x.reshape(n, d)
    pos = positions.reshape(n, 1)
    invf = _rope_lane_freqs()
    row = lambda v: v.reshape(1, -1)
    for i in range(depth):
        wb, wf, wqa, wqb, wk, wv = _layer_weights(w_in[i], w_uq[i], w_ukv[i])
        ob, of = _inproj(h, row(norm_mix_g[i]), wb, wf)
        bias = jnp.pad(b_forget[i], (0, LANES - N_HEADS)).reshape(1, LANES)
        fqk = _foxprep(ob, of, bias, batch, s_len)
        y_fox = _softmax_attn(fqk, 0, fqk, 1, ob, OB_FV // (GW // 2), batch, s_len, True)
        y_sb = _sb_attn(ob, batch, s_len)
        mq, mk, mv = _mla_proj(of, pos, invf, row(mla_q_norm_g[i]), row(mla_kv_norm_g[i]), wqa, wqb, wk, wv)
        y_mla = _softmax_attn(mq, 0, mk, 0, mv, 0, batch, s_len, False)
        h = _outproj(y_fox, y_sb, y_mla, row(mix_out_norm_g[i]), w_o[i].astype(BF16), h)
        sk = peer_sub_keys[i].reshape(2 * PEER_HEADS, PEER_NKEYS, PEER_HALF)
        wqt = peer_w_query[i].T
        wqt_hi = wqt.astype(BF16)
        wqt_lo = (wqt - wqt_hi.astype(F32)).astype(BF16)
        h = _peer(h, row(norm_ffn_g[i]), wqt_hi, wqt_lo, sk, peer_u[i].astype(BF16), peer_v[i].T.astype(BF16))
        h = _ple(h, p[i].reshape(n, PLE_DIM), row(ple_norm_g[i]), w_ple_gate[i].astype(BF16),
                 w_ple[i].astype(BF16), row(final_norm_g), i == depth - 1)
    return h.reshape(batch, s_len, d)
```

```python
import functools
import math

import numpy as np
import jax
import jax.numpy as jnp
from jax import lax
from jax.experimental import pallas as pl
from jax.experimental.pallas import tpu as pltpu

F32 = jnp.float32
BF16 = jnp.bfloat16

D_MODEL = 1024
HEAD_DIM = 64
N_HEADS = 4
MLA_Q_RANK = 256
MLA_KV_RANK = 128
MLA_NOPE = 64
MLA_ROPE = 32
MLA_V = 128
ROPE_THETA = 10000.0
IN_SPLITS = (256, 256, 256, 4, 256, 256, 256, MLA_Q_RANK, MLA_KV_RANK, MLA_ROPE)
PEER_HEADS = 8
PEER_NKEYS = 128
PEER_HALF = 64
PEER_TOPK = 16
PLE_DIM = 256
EPS = 1e-6

LANES = 128
SLOT = LANES
VMEM_LIMIT = 56 * 1024 * 1024

TM_PROJ = 512
TQ_SOFTMAX = 256
TQ_SB = 256
TK_SB = 128
T_PEER = 512
EB_PEER = 512
MXU_PIECES = 2

OB_FQ, OB_FK, OB_SQ, OB_SK, OB_FV, OB_SV, OB_W = 0, 512, 1024, 1536, 2048, 2304, 2560
GW = N_HEADS * SLOT
OF_CQ, OF_CKV, OF_KRA, OF_KRB, OF_FL, OF_W = 0, 256, 384, 512, 640, 768

PEER_PAIRS = tuple((a, b) for a in range(PEER_TOPK) for b in range(PEER_TOPK) if (a + 1) * (b + 1) <= PEER_TOPK)
N_CAND = len(PEER_PAIRS)
N_CAND_PAD = 56


def _cparams(sem):
    return pltpu.CompilerParams(dimension_semantics=sem, vmem_limit_bytes=VMEM_LIMIT)


def _rms(x, g):
    return x * lax.rsqrt(jnp.mean(x * x, axis=-1, keepdims=True) + EPS) * g


def _dot(a, b):
    return jnp.dot(a, b, preferred_element_type=F32)


def _dot_nt(a, b):
    return lax.dot_general(a, b, (((1,), (1,)), ((), ())), preferred_element_type=F32)


def _dot_hi(a, b):
    return jnp.dot(a, b, preferred_element_type=F32, precision=lax.Precision.HIGHEST)


def _inproj_kernel(x_ref, g_ref, wb_ref, wf_ref, ob_ref, of_ref):
    xb = _rms(x_ref[...], g_ref[...]).astype(BF16)
    ob_ref[...] = _dot(xb, wb_ref[...]).astype(BF16)
    of_ref[...] = _dot(xb, wf_ref[...])


def _inproj(h, g, wb, wf):
    n = h.shape[0]
    tm = TM_PROJ
    return pl.pallas_call(
        _inproj_kernel,
        grid=(n // tm,),
        in_specs=[
            pl.BlockSpec((tm, D_MODEL), lambda i: (i, 0)),
            pl.BlockSpec((1, D_MODEL), lambda i: (0, 0)),
            pl.BlockSpec((D_MODEL, OB_W), lambda i: (0, 0)),
            pl.BlockSpec((D_MODEL, OF_W), lambda i: (0, 0)),
        ],
        out_specs=[
            pl.BlockSpec((tm, OB_W), lambda i: (i, 0)),
            pl.BlockSpec((tm, OF_W), lambda i: (i, 0)),
        ],
        out_shape=[jax.ShapeDtypeStruct((n, OB_W), BF16), jax.ShapeDtypeStruct((n, OF_W), F32)],
        compiler_params=_cparams(("parallel",)),
        name="inproj",
    )(h, g, wb, wf)


def _split3(x):
    hi = x.astype(BF16)
    r = x - hi.astype(F32)
    mid = r.astype(BF16)
    lo = (r - mid.astype(F32)).astype(BF16)
    return hi, mid, lo


def _foxprep_kernel(qk_ref, fl_ref, b_ref, o_ref, f_ref, *, rb):
    s_len = fl_ref.shape[0]
    x = fl_ref[...] + b_ref[...]
    logf = jnp.minimum(x, 0.0) - jnp.log(1.0 + jnp.exp(-jnp.abs(x)))
    hi, mid, lo = _split3(logf)
    for r in range(s_len // rb):
        row = lax.broadcasted_iota(jnp.int32, (rb, s_len), 0) + r * rb
        col = lax.broadcasted_iota(jnp.int32, (rb, s_len), 1)
        tri = jnp.where(col <= row, 1.0, 0.0).astype(BF16)
        f_ref[r * rb:(r + 1) * rb, :] = _dot(tri, hi) + _dot(tri, mid) + _dot(tri, lo)
    f = f_ref[...]
    lane = lax.broadcasted_iota(jnp.int32, (s_len, LANES), 1)
    for h in range(N_HEADS):
        fh = f[:, h:h + 1]
        fhi, fmid, flo = (t.astype(F32) for t in _split3(fh))
        qa = jnp.where(lane == 64, fhi, jnp.where(lane == 65, fmid, jnp.where(lane == 66, flo,
             jnp.where((lane >= 67) & (lane < 70), 1.0, 0.0))))
        ka = jnp.where(lane == 67, -fhi, jnp.where(lane == 68, -fmid, jnp.where(lane == 69, -flo,
             jnp.where((lane >= 64) & (lane < 67), 1.0, 0.0))))
        qs = slice(h * SLOT, (h + 1) * SLOT)
        ks = slice(N_HEADS * SLOT + h * SLOT, N_HEADS * SLOT + (h + 1) * SLOT)
        o_ref[:, qs] = (qk_ref[:, qs].astype(F32) + qa).astype(BF16)
        o_ref[:, ks] = (qk_ref[:, ks].astype(F32) + ka).astype(BF16)


def _foxprep(ob, of, bias, batch, s_len):
    n = ob.shape[0]
    w = 2 * N_HEADS * SLOT
    return pl.pallas_call(
        functools.partial(_foxprep_kernel, rb=256),
        grid=(batch,),
        in_specs=[
            pl.BlockSpec((s_len, w), lambda b: (b, 0)),
            pl.BlockSpec((s_len, LANES), lambda b: (b, OF_FL // LANES)),
            pl.BlockSpec((1, LANES), lambda b: (0, 0)),
        ],
        out_specs=pl.BlockSpec((s_len, w), lambda b: (b, 0)),
        out_shape=jax.ShapeDtypeStruct((n, w), BF16),
        scratch_shapes=[pltpu.VMEM((s_len, LANES), F32)],
        compiler_params=_cparams(("parallel",)),
        name="foxprep",
    )(ob, of, bias)


def _softmax_attn_kernel(q_ref, k_ref, v_ref, o_ref, *, tq, pair_sum):
    qi = pl.program_id(1)
    row = lax.broadcasted_iota(jnp.int32, (tq, tq), 0)
    col = lax.broadcasted_iota(jnp.int32, (tq, tq), 1)
    causal = col <= row
    slots = [slice(h * SLOT, (h + 1) * SLOT) for h in range(N_HEADS)]
    qs = [q_ref[:, hs] for hs in slots]

    ones = jnp.ones((tq, SLOT), BF16)

    def block(j, carry, masked):
        rows = pl.ds(pl.multiple_of(j * tq, tq), tq)
        ss = [_dot_nt(qs[h], k_ref[rows, slots[h]]) for h in range(N_HEADS)]
        ms, alphas, ps = [], [], []
        for h in range(N_HEADS):
            s = jnp.where(causal, ss[h], -1e30) if masked else ss[h]
            m_new = jnp.maximum(carry[h][0], jnp.max(s, axis=1, keepdims=True))
            ms.append(m_new)
            alphas.append(jnp.exp(carry[h][0] - m_new))
            ps.append(jnp.exp(s - m_new).astype(BF16))
        if pair_sum:
            pvs = []
            for pr in range(N_HEADS // 2):
                vx = jnp.concatenate([v_ref[rows, slots[pr]], ones], axis=1)
                both = _dot(jnp.concatenate([ps[2 * pr], ps[2 * pr + 1]], axis=0), vx)
                pvs += [both[:tq], both[tq:]]
        else:
            pvs = [_dot(ps[h], jnp.concatenate([v_ref[rows, slots[h]], ones], axis=1)) for h in range(N_HEADS)]
        return tuple((ms[h], alphas[h] * carry[h][1] + pvs[h][:, SLOT:], alphas[h] * carry[h][2] + pvs[h][:, :SLOT])
                     for h in range(N_HEADS))

    init = tuple((jnp.full((tq, 1), -1e30, F32), jnp.zeros((tq, SLOT), F32), jnp.zeros((tq, SLOT), F32))
                 for _ in range(N_HEADS))
    carry = lax.fori_loop(0, qi, lambda j, c: block(j, c, False), init)
    outs = [acc / l for _, l, acc in block(qi, carry, True)]
    if pair_sum:
        lane = lax.broadcasted_iota(jnp.int32, (tq, SLOT), 1)
        for pr in range(N_HEADS // 2):
            o_ref[:, slots[pr]] = jnp.where(lane < HEAD_DIM, outs[2 * pr], outs[2 * pr + 1])
    else:
        for h in range(N_HEADS):
            o_ref[:, slots[h]] = outs[h]


def _softmax_attn(q_arr, q_col, k_arr, k_col, v_arr, v_col, batch, s_len, pair_sum):
    n = q_arr.shape[0]
    tq = TQ_SOFTMAX
    nq = s_len // tq
    vw = GW // 2 if pair_sum else GW
    return pl.pallas_call(
        functools.partial(_softmax_attn_kernel, tq=tq, pair_sum=pair_sum),
        grid=(batch, nq),
        in_specs=[
            pl.BlockSpec((tq, GW), lambda b, i: (b * nq + i, q_col)),
            pl.BlockSpec((s_len, GW), lambda b, i: (b, k_col)),
            pl.BlockSpec((s_len, vw), lambda b, i: (b, v_col)),
        ],
        out_specs=pl.BlockSpec((tq, vw), lambda b, i: (b * nq + i, 0)),
        out_shape=jax.ShapeDtypeStruct((n, vw), F32),
        compiler_params=_cparams(("parallel", "arbitrary")),
        name="fox_attn" if pair_sum else "mla_attn",
    )(q_arr, k_arr, v_arr)


def _sb_attn_kernel(q_ref, k_ref, v_ref, o_ref, *, tq, tk):
    qi = pl.program_id(1)
    nz = tq // tk
    row = lax.broadcasted_iota(jnp.int32, (tq, tk), 0)
    col = lax.broadcasted_iota(jnp.int32, (tq, tk), 1)
    ur = lax.broadcasted_iota(jnp.int32, (tk, 2 * tk), 0)
    uc = lax.broadcasted_iota(jnp.int32, (tk, 2 * tk), 1)
    uo = jnp.where((uc >= tk) | (ur > uc), 1.0, 0.0).astype(BF16)
    slots = [slice(h * SLOT, (h + 1) * SLOT) for h in range(N_HEADS)]
    qs = [q_ref[:, hs] for hs in slots]

    lane = lax.broadcasted_iota(jnp.int32, (tq, SLOT), 1)

    def block(j, carry, zone):
        cs, accs = carry
        rows = pl.ds(pl.multiple_of(j * tk, tk), tk)
        zs = [_dot_nt(qs[h], k_ref[rows, slots[h]]) for h in range(N_HEADS)]
        strict = None if zone is None else (col + zone * tk) < row
        lbs, his, los = [], [], []
        for z in zs:
            lb = jnp.minimum(z, 0.0) - jnp.log(1.0 + jnp.exp(-jnp.abs(z)))
            lom = lb - z
            if strict is not None:
                lom = jnp.where(strict, lom, 0.0)
            hi = lom.astype(BF16)
            lbs.append(lb)
            his.append(hi)
            los.append((lom - hi.astype(F32)).astype(BF16))
        r = _dot(jnp.concatenate(his + los, axis=0), uo)
        ws, new_cs = [], []
        for h in range(N_HEADS):
            rh = r[h * tq:(h + 1) * tq] + r[(N_HEADS + h) * tq:(N_HEADS + h + 1) * tq]
            w = jnp.exp(lbs[h] + rh[:, :tk] + cs[h])
            if strict is not None:
                w = jnp.where(strict, w, 0.0)
            ws.append(w.astype(BF16))
            new_cs.append(cs[h] + rh[:, tk:])
        new_accs = []
        for pr in range(N_HEADS // 2):
            both = _dot(jnp.concatenate([ws[2 * pr], ws[2 * pr + 1]], axis=0), v_ref[rows, slots[pr]])
            new_accs.append(accs[pr] + jnp.where(lane < HEAD_DIM, both[:tq], both[tq:]))
        return tuple(new_cs), tuple(new_accs)

    carry = (tuple(jnp.zeros((tq, tk), F32) for _ in range(N_HEADS)),
             tuple(jnp.zeros((tq, SLOT), F32) for _ in range(N_HEADS // 2)))
    for zi in reversed(range(nz)):
        carry = block(qi * nz + zi, carry, zi)
    n_full = qi * nz
    carry = lax.fori_loop(0, n_full, lambda jj, cr: block(n_full - 1 - jj, cr, None), carry)
    for pr in range(N_HEADS // 2):
        o_ref[:, slots[pr]] = carry[1][pr]


def _sb_attn(ob, batch, s_len):
    n = ob.shape[0]
    tq, tk = TQ_SB, TK_SB
    nq = s_len // tq
    vw = GW // 2
    return pl.pallas_call(
        functools.partial(_sb_attn_kernel, tq=tq, tk=tk),
        grid=(batch, nq),
        in_specs=[
            pl.BlockSpec((tq, GW), lambda b, i: (b * nq + i, OB_SQ // GW)),
            pl.BlockSpec((s_len, GW), lambda b, i: (b, OB_SK // GW)),
            pl.BlockSpec((s_len, vw), lambda b, i: (b, OB_SV // vw)),
        ],
        out_specs=pl.BlockSpec((tq, vw), lambda b, i: (b * nq + i, 0)),
        out_shape=jax.ShapeDtypeStruct((n, vw), F32),
        compiler_params=_cparams(("parallel", "arbitrary")),
        name="sb_attn",
    )(ob, ob, ob)


def _mla_proj_kernel(cq_ref, ckv_ref, kra_ref, krb_ref, pos_ref, invf_ref, gq_ref, gkv_ref,
                     wqa_ref, wqb_ref, wk_ref, wv_ref, q_ref, k_ref, v_ref, *, scale):
    ang = pos_ref[...].astype(F32) * invf_ref[...]
    cs, sn = jnp.cos(ang), jnp.sin(ang)
    cqn = _rms(cq_ref[...], gq_ref[...]).astype(BF16)
    ckvn = _rms(ckv_ref[...], gkv_ref[...]).astype(BF16)
    qa, qb = _dot(cqn, wqa_ref[...]), _dot(cqn, wqb_ref[...])
    kn = _dot(ckvn, wk_ref[...])
    v_ref[...] = _dot(ckvn, wv_ref[...]).astype(BF16)
    kr = kra_ref[...] * cs + krb_ref[...] * sn
    for h in range(N_HEADS):
        hs = slice(h * SLOT, (h + 1) * SLOT)
        q_ref[:, hs] = ((qa[:, hs] * cs + qb[:, hs] * sn) * scale).astype(BF16)
        k_ref[:, hs] = (kn[:, hs] + kr).astype(BF16)


def _mla_proj(of, pos, invf, gq, gkv, wqa, wqb, wk, wv):
    n = of.shape[0]
    tm = TM_PROJ
    w = N_HEADS * SLOT
    const = lambda i: (0, 0)
    return pl.pallas_call(
        functools.partial(_mla_proj_kernel, scale=(MLA_NOPE + MLA_ROPE) ** -0.5),
        grid=(n // tm,),
        in_specs=[
            pl.BlockSpec((tm, MLA_Q_RANK), lambda i: (i, OF_CQ // MLA_Q_RANK)),
            pl.BlockSpec((tm, LANES), lambda i: (i, OF_CKV // LANES)),
            pl.BlockSpec((tm, LANES), lambda i: (i, OF_KRA // LANES)),
            pl.BlockSpec((tm, LANES), lambda i: (i, OF_KRB // LANES)),
            pl.BlockSpec((tm, 1), lambda i: (i, 0)),
            pl.BlockSpec((1, LANES), const),
            pl.BlockSpec((1, MLA_Q_RANK), const),
            pl.BlockSpec((1, MLA_KV_RANK), const),
            pl.BlockSpec((MLA_Q_RANK, w), const),
            pl.BlockSpec((MLA_Q_RANK, w), const),
            pl.BlockSpec((MLA_KV_RANK, w), const),
            pl.BlockSpec((MLA_KV_RANK, w), const),
        ],
        out_specs=[pl.BlockSpec((tm, w), lambda i: (i, 0))] * 3,
        out_shape=[jax.ShapeDtypeStruct((n, w), BF16)] * 3,
        compiler_params=_cparams(("parallel",)),
        name="mla_proj",
    )(of, of, of, of, pos, invf, gq, gkv, wqa, wqb, wk, wv)


def _outproj_kernel(yf_ref, ys_ref, ym_ref, g_ref, wo_ref, h_ref, o_ref):
    gw = yf_ref.shape[1]
    nf = _rms(yf_ref[...], g_ref[:, 0:gw]).astype(BF16)
    ns = _rms(ys_ref[...], g_ref[:, gw:2 * gw]).astype(BF16)
    nm = _rms(ym_ref[...], g_ref[:, 2 * gw:]).astype(BF16)
    acc = _dot(nf, wo_ref[0:gw, :]) + _dot(ns, wo_ref[gw:2 * gw, :]) + _dot(nm, wo_ref[2 * gw:, :])
    o_ref[...] = h_ref[...] + acc


def _outproj(yf, ys, ym, g, wo, h):
    n = h.shape[0]
    tm = TM_PROJ
    const = lambda i: (0, 0)
    return pl.pallas_call(
        _outproj_kernel,
        grid=(n // tm,),
        in_specs=[
            pl.BlockSpec((tm, yf.shape[1]), lambda i: (i, 0)),
            pl.BlockSpec((tm, ys.shape[1]), lambda i: (i, 0)),
            pl.BlockSpec((tm, ym.shape[1]), lambda i: (i, 0)),
            pl.BlockSpec((1, D_MODEL), const),
            pl.BlockSpec((D_MODEL, D_MODEL), const),
            pl.BlockSpec((tm, D_MODEL), lambda i: (i, 0)),
        ],
        out_specs=pl.BlockSpec((tm, D_MODEL), lambda i: (i, 0)),
        out_shape=jax.ShapeDtypeStruct((n, D_MODEL), F32),
        compiler_params=_cparams(("parallel",)),
        name="outproj",
    )(yf, ys, ym, g, wo, h)


def _top_rows(e, n):
    rows = []
    for r in range(n):
        m = jnp.max(e, axis=0, keepdims=True)
        rows.append(m)
        if r + 1 < n:
            e = jnp.where(e == m, -1.0, e)
    return rows


def _peer_kernel(*refs, tile, eb, ns):
    h_ref, g_ref, wqh_ref, wql_ref, sk_ref = refs[:5]
    u_refs, vt_refs = refs[5:5 + MXU_PIECES], refs[5 + MXU_PIECES:5 + 2 * MXU_PIECES]
    (o_ref, xnt_ref, e2_ref, e1s_ref, g16_ref, tv_ref, cand_ref, candn_ref,
     pre0_ref, pre1_ref, act0_ref, act1_ref, acc_ref) = refs[5 + 2 * MXU_PIECES:]
    s = pl.program_id(1)

    @pl.when(s == 0)
    def _():
        xn = _rms(h_ref[...], g_ref[...])
        xnt = xn.T
        x_hi = xnt.astype(BF16)
        x_lo = (xnt - x_hi.astype(F32)).astype(BF16)
        xnt_ref[...] = x_hi
        qt = _dot(wqh_ref[...], x_hi) + (_dot(wqh_ref[...], x_lo) + _dot(wql_ref[...], x_hi))
        cand_ref[N_CAND:, :] = jnp.full((N_CAND_PAD - N_CAND, tile), -1.0, F32)
        candn_ref[N_CAND:, :] = jnp.full((N_CAND_PAD - N_CAND, tile), -1.0, F32)
        for hd in range(PEER_HEADS):
            es = []
            for p in range(2):
                base = (hd * 2 + p) * PEER_HALF
                sc = _dot_hi(sk_ref[hd * 2 + p], qt[base:base + PEER_HALF, :])
                e = jnp.exp(sc - jnp.max(sc, axis=0, keepdims=True))
                es.append(e)
                for r, m in enumerate(_top_rows(e, PEER_TOPK)):
                    tv_ref[p, r:r + 1, :] = m
            for i, (a, b) in enumerate(PEER_PAIRS):
                cand_ref[i:i + 1, :] = tv_ref[0, a:a + 1, :] * tv_ref[1, b:b + 1, :]
            cand = cand_ref[...]
            tau = _top_rows(cand, PEER_TOPK)[-1]
            sel = cand >= tau
            inv_z = 1.0 / jnp.sum(jnp.where(sel, cand, 0.0), axis=0, keepdims=True)
            e1s_ref[hd] = es[0] * inv_z
            for lg in range(tile // LANES):
                e2_ref[hd, lg] = es[1][:, lg * LANES:(lg + 1) * LANES]
            t1s = tv_ref[0] * inv_z
            for i, (a, b) in enumerate(PEER_PAIRS):
                candn_ref[i:i + 1, :] = t1s[a:a + 1, :] * tv_ref[1, b:b + 1, :]
            g16_ref[hd] = jnp.min(jnp.where(sel, candn_ref[...], jnp.inf), axis=0, keepdims=True)
        for ref in (pre0_ref, pre1_ref, act0_ref, act1_ref, acc_ref):
            ref[...] = jnp.zeros_like(ref)

    npc = eb // PEER_NKEYS
    nlg = tile // LANES
    n1_base = jnp.clip(s - 1, 0, ns - 1) * npc
    pre_refs, act_refs = (pre0_ref, pre1_ref), (act0_ref, act1_ref)

    def stage_act(a, prv, lane_groups):
        rc = PEER_NKEYS // 2
        e1full = [e1s_ref[hd, pl.ds(n1_base + a, 1), :] for hd in range(PEER_HEADS)]
        for lg in lane_groups:
            cols = slice(lg * LANES, (lg + 1) * LANES)
            e1rows = [row[:, cols] for row in e1full]
            g16rows = [g16_ref[hd, :, cols] for hd in range(PEER_HEADS)]
            for r0 in range(0, PEER_NKEYS, rc):
                gate = None
                for hd in range(PEER_HEADS):
                    val = e2_ref[hd, lg, r0:r0 + rc, :] * e1rows[hd]
                    contrib = jnp.where(val >= g16rows[hd], val, 0.0)
                    gate = contrib if gate is None else gate + contrib
                rows = slice(a * PEER_NKEYS + r0, a * PEER_NKEYS + r0 + rc)
                pa = pre_refs[prv][lg, rows, :]
                gelu = 0.5 * pa * (1.0 + lax.erf(pa * (1.0 / math.sqrt(2.0))))
                act_refs[prv][lg, rows, :] = (gate * gelu).astype(BF16)

    def step(cur):
        prv = 1 - cur
        pre_rows, out_rows = eb // MXU_PIECES, D_MODEL // MXU_PIECES

        def pre_piece(r):
            rows = slice(r * pre_rows, (r + 1) * pre_rows)
            pre = _dot(u_refs[r][...], xnt_ref[...])
            for lg in range(nlg):
                pre_refs[cur][lg, rows, :] = pre[:, lg * LANES:(lg + 1) * LANES]

        def out_piece(r):
            rows = slice(r * out_rows, (r + 1) * out_rows)
            act = jnp.concatenate([act_refs[cur][lg] for lg in range(nlg)], axis=1)
            acc_ref[rows, :] += _dot(vt_refs[r][...], act)

        mxu = [functools.partial(fn, r) for r in range(MXU_PIECES) for fn in (pre_piece, out_piece)]
        vpu = [functools.partial(stage_act, a, prv, (lg,)) for a in range(npc) for lg in range(tile // LANES)]
        done = 0
        for k, piece in enumerate(mxu):
            piece()
            upto = (k + 1) * len(vpu) // len(mxu)
            for fn in vpu[done:upto]:
                fn()
            done = upto

    for parity in range(2):
        pl.when(s % 2 == parity)(functools.partial(step, parity))

    @pl.when(s == pl.num_programs(1) - 1)
    def _():
        o_ref[...] = h_ref[...] + acc_ref[...].T


def _peer(h, g, wqt_hi, wqt_lo, sk, u, v):
    n = h.shape[0]
    n_exp = u.shape[0]
    tile, eb = T_PEER, EB_PEER
    ns = n_exp // eb
    dq = wqt_hi.shape[0]
    vt = v.reshape(ns, eb, D_MODEL).transpose(0, 2, 1).astype(BF16)
    pieces = range(MXU_PIECES)
    u_specs = [pl.BlockSpec((eb // MXU_PIECES, D_MODEL),
                            lambda i, s, r=r: (jnp.minimum(s, ns - 1) * MXU_PIECES + r, 0)) for r in pieces]
    vt_specs = [pl.BlockSpec((None, D_MODEL // MXU_PIECES, eb),
                             lambda i, s, r=r: (jnp.clip(s - 2, 0, ns - 1), r, 0)) for r in pieces]
    return pl.pallas_call(
        functools.partial(_peer_kernel, tile=tile, eb=eb, ns=ns),
        grid=(n // tile, ns + 2),
        in_specs=[
            pl.BlockSpec((tile, D_MODEL), lambda i, s: (i, 0)),
            pl.BlockSpec((1, D_MODEL), lambda i, s: (0, 0)),
            pl.BlockSpec((dq, D_MODEL), lambda i, s: (0, 0)),
            pl.BlockSpec((dq, D_MODEL), lambda i, s: (0, 0)),
            pl.BlockSpec((2 * PEER_HEADS, PEER_NKEYS, PEER_HALF), lambda i, s: (0, 0, 0)),
            *u_specs,
            *vt_specs,
        ],
        out_specs=pl.BlockSpec((tile, D_MODEL), lambda i, s: (i, 0)),
        out_shape=jax.ShapeDtypeStruct((n, D_MODEL), F32),
        scratch_shapes=[
            pltpu.VMEM((D_MODEL, tile), BF16),
            pltpu.VMEM((PEER_HEADS, tile // LANES, PEER_NKEYS, LANES), F32),
            pltpu.VMEM((PEER_HEADS, PEER_NKEYS, tile), F32),
            pltpu.VMEM((PEER_HEADS, 1, tile), F32),
            pltpu.VMEM((2, PEER_TOPK, tile), F32),
            pltpu.VMEM((N_CAND_PAD, tile), F32),
            pltpu.VMEM((N_CAND_PAD, tile), F32),
            pltpu.VMEM((tile // LANES, eb, LANES), F32),
            pltpu.VMEM((tile // LANES, eb, LANES), F32),
            pltpu.VMEM((tile // LANES, eb, LANES), BF16),
            pltpu.VMEM((tile // LANES, eb, LANES), BF16),
            pltpu.VMEM((D_MODEL, tile), F32),
        ],
        compiler_params=_cparams(("parallel", "arbitrary")),
        name="peer",
    )(h, g, wqt_hi, wqt_lo, sk, *([u] * MXU_PIECES), *([vt] * MXU_PIECES))


def _ple_kernel(h_ref, p_ref, g_ref, wg_ref, wp_ref, fg_ref, o_ref, *, final):
    h = h_ref[...]
    gate = jax.nn.sigmoid(_dot(_rms(h, g_ref[...]).astype(BF16), wg_ref[...]))
    out = h + gate * _dot(p_ref[...].astype(BF16), wp_ref[...])
    o_ref[...] = _rms(out, fg_ref[...]) if final else out


def _ple(h, p, g, wg, wp, fg, final):
    n = h.shape[0]
    tm = TM_PROJ
    const = lambda i: (0, 0)
    return pl.pallas_call(
        functools.partial(_ple_kernel, final=final),
        grid=(n // tm,),
        in_specs=[
            pl.BlockSpec((tm, D_MODEL), lambda i: (i, 0)),
            pl.BlockSpec((tm, PLE_DIM), lambda i: (i, 0)),
            pl.BlockSpec((1, D_MODEL), const),
            pl.BlockSpec((D_MODEL, D_MODEL), const),
            pl.BlockSpec((PLE_DIM, D_MODEL), const),
            pl.BlockSpec((1, D_MODEL), const),
        ],
        out_specs=pl.BlockSpec((tm, D_MODEL), lambda i: (i, 0)),
        out_shape=jax.ShapeDtypeStruct((n, D_MODEL), F32),
        compiler_params=_cparams(("parallel",)),
        name="ple",
    )(h, p, g, wg, wp, fg)


def _pad_heads(w, width=SLOT):
    k = w.shape[0]
    w = w.reshape(k, N_HEADS, -1)
    return jnp.pad(w, ((0, 0), (0, 0), (0, width - w.shape[2]))).reshape(k, N_HEADS * width)


def _rot_partner(w_rot):
    half = MLA_ROPE // 2
    return jnp.concatenate([-w_rot[..., half:], w_rot[..., :half]], axis=-1)


def _layer_weights(w_in, w_uq, w_ukv):
    offs = [int(o) for o in np.cumsum(IN_SPLITS)[:-1]]
    fq, fk, fv, fl, sq, sk, sv, cq, ckv, kr = jnp.split(w_in, offs, axis=1)
    att_scale = HEAD_DIM ** -0.5
    wb = jnp.concatenate([_pad_heads(fq * att_scale), _pad_heads(fk),
                          _pad_heads(sq * att_scale), _pad_heads(sk), fv, sv], axis=1).astype(BF16)
    z = lambda c: jnp.zeros((D_MODEL, c), F32)
    kra = jnp.concatenate([z(MLA_NOPE), kr, z(SLOT - MLA_NOPE - MLA_ROPE)], axis=1)
    krb = jnp.concatenate([z(MLA_NOPE), _rot_partner(kr), z(SLOT - MLA_NOPE - MLA_ROPE)], axis=1)
    wf = jnp.concatenate([cq, ckv, kra, krb, fl, z(LANES - N_HEADS)], axis=1).astype(BF16)
    uq = w_uq.reshape(MLA_Q_RANK, N_HEADS, MLA_NOPE + MLA_ROPE)
    zq = jnp.zeros((MLA_Q_RANK, N_HEADS, SLOT - MLA_NOPE - MLA_ROPE), F32)
    wqa = jnp.concatenate([uq, zq], axis=2).reshape(MLA_Q_RANK, N_HEADS * SLOT).astype(BF16)
    wqb = jnp.concatenate([jnp.zeros_like(uq[..., :MLA_NOPE]), _rot_partner(uq[..., MLA_NOPE:]), zq],
                          axis=2).reshape(MLA_Q_RANK, N_HEADS * SLOT).astype(BF16)
    ukv = w_ukv.reshape(MLA_KV_RANK, N_HEADS, MLA_NOPE + MLA_V)
    wk = jnp.pad(ukv[..., :MLA_NOPE], ((0, 0), (0, 0), (0, SLOT - MLA_NOPE))).reshape(MLA_KV_RANK, -1).astype(BF16)
    wv = ukv[..., MLA_NOPE:].reshape(MLA_KV_RANK, -1).astype(BF16)
    return wb, wf, wqa, wqb, wk, wv


def _rope_lane_freqs():
    half = MLA_ROPE // 2
    inv_freq = ROPE_THETA ** (-jnp.arange(half, dtype=F32) / half)
    zeros = lambda c: jnp.zeros((c,), F32)
    return jnp.concatenate([zeros(MLA_NOPE), inv_freq, inv_freq, zeros(SLOT - MLA_NOPE - MLA_ROPE)]).reshape(1, SLOT)


def kernel(x, p, positions, norm_mix_g, w_in, b_forget, mla_q_norm_g, w_uq, mla_kv_norm_g, w_ukv, mix_out_norm_g, w_o, norm_ffn_g, peer_w_query, peer_sub_keys, peer_u, peer_v, w_ple, ple_norm_g, w_ple_gate, final_norm_g):
    batch, s_len, d = x.shape
    depth = p.shape[0]
    n = batch * s_len
    h = x.reshape(n, d)
    pos = positions.reshape(n, 1)
    invf = _rope_lane_freqs()
    row = lambda v: v.reshape(1, -1)
    for i in range(depth):
        wb, wf, wqa, wqb, wk, wv = _layer_weights(w_in[i], w_uq[i], w_ukv[i])
        ob, of = _inproj(h, row(norm_mix_g[i]), wb, wf)
        bias = jnp.pad(b_forget[i], (0, LANES - N_HEADS)).reshape(1, LANES)
        fqk = _foxprep(ob, of, bias, batch, s_len)
        y_fox = _softmax_attn(fqk, 0, fqk, 1, ob, OB_FV // (GW // 2), batch, s_len, True)
        y_sb = _sb_attn(ob, batch, s_len)
        mq, mk, mv = _mla_proj(of, pos, invf, row(mla_q_norm_g[i]), row(mla_kv_norm_g[i]), wqa, wqb, wk, wv)
        y_mla = _softmax_attn(mq, 0, mk, 0, mv, 0, batch, s_len, False)
        h = _outproj(y_fox, y_sb, y_mla, row(mix_out_norm_g[i]), w_o[i].astype(BF16), h)
        sk = peer_sub_keys[i].reshape(2 * PEER_HEADS, PEER_NKEYS, PEER_HALF)
        wqt = peer_w_query[i].T
        wqt_hi = wqt.astype(BF16)
        wqt_lo = (wqt - wqt_hi.astype(F32)).astype(BF16)
        h = _peer(h, row(norm_ffn_g[i]), wqt_hi, wqt_lo, sk, peer_u[i].astype(BF16), peer_v[i])
        h = _ple(h, p[i].reshape(n, PLE_DIM), row(ple_norm_g[i]), w_ple_gate[i].astype(BF16),
                 w_ple[i].astype(BF16), row(final_norm_g), i == depth - 1)
    return h.reshape(batch, s_len, d)
```

```python
import functools
import math

import numpy as np
import jax
import jax.numpy as jnp
from jax import lax
from jax.experimental import pallas as pl
from jax.experimental.pallas import tpu as pltpu

F32 = jnp.float32
BF16 = jnp.bfloat16

D_MODEL = 1024
HEAD_DIM = 64
N_HEADS = 4
MLA_Q_RANK = 256
MLA_KV_RANK = 128
MLA_NOPE = 64
MLA_ROPE = 32
MLA_V = 128
ROPE_THETA = 10000.0
IN_SPLITS = (256, 256, 256, 4, 256, 256, 256, MLA_Q_RANK, MLA_KV_RANK, MLA_ROPE)
PEER_HEADS = 8
PEER_NKEYS = 128
PEER_HALF = 64
PEER_TOPK = 16
PLE_DIM = 256
EPS = 1e-6

LANES = 128
SLOT = LANES
VMEM_LIMIT = 56 * 1024 * 1024

TM_PROJ = 512
TQ_SOFTMAX = 256
TQ_SB = 256
TK_SB = 128
T_PEER = 512
EB_PEER = 512
MXU_PIECES = 2

OB_FQ, OB_FK, OB_SQ, OB_SK, OB_FV, OB_SV, OB_W = 0, 512, 1024, 1536, 2048, 2304, 2560
GW = N_HEADS * SLOT
OF_CQ, OF_CKV, OF_KRA, OF_KRB, OF_FL, OF_W = 0, 256, 384, 512, 640, 768

PEER_PAIRS = tuple((a, b) for a in range(PEER_TOPK) for b in range(PEER_TOPK) if (a + 1) * (b + 1) <= PEER_TOPK)
N_CAND = len(PEER_PAIRS)
N_CAND_PAD = 56


def _cparams(sem):
    return pltpu.CompilerParams(dimension_semantics=sem, vmem_limit_bytes=VMEM_LIMIT)


def _rms(x, g):
    return x * lax.rsqrt(jnp.mean(x * x, axis=-1, keepdims=True) + EPS) * g


def _dot(a, b):
    return jnp.dot(a, b, preferred_element_type=F32)


def _dot_nt(a, b):
    return lax.dot_general(a, b, (((1,), (1,)), ((), ())), preferred_element_type=F32)


def _dot_hi(a, b):
    return jnp.dot(a, b, preferred_element_type=F32, precision=lax.Precision.HIGHEST)


def _inproj_kernel(x_ref, g_ref, wb_ref, wf_ref, ob_ref, of_ref):
    xb = _rms(x_ref[...], g_ref[...]).astype(BF16)
    ob_ref[...] = _dot(xb, wb_ref[...]).astype(BF16)
    of_ref[...] = _dot(xb, wf_ref[...])


def _inproj(h, g, wb, wf):
    n = h.shape[0]
    tm = TM_PROJ
    return pl.pallas_call(
        _inproj_kernel,
        grid=(n // tm,),
        in_specs=[
            pl.BlockSpec((tm, D_MODEL), lambda i: (i, 0)),
            pl.BlockSpec((1, D_MODEL), lambda i: (0, 0)),
            pl.BlockSpec((D_MODEL, OB_W), lambda i: (0, 0)),
            pl.BlockSpec((D_MODEL, OF_W), lambda i: (0, 0)),
        ],
        out_specs=[
            pl.BlockSpec((tm, OB_W), lambda i: (i, 0)),
            pl.BlockSpec((tm, OF_W), lambda i: (i, 0)),
        ],
        out_shape=[jax.ShapeDtypeStruct((n, OB_W), BF16), jax.ShapeDtypeStruct((n, OF_W), F32)],
        compiler_params=_cparams(("parallel",)),
        name="inproj",
    )(h, g, wb, wf)


def _split3(x):
    hi = x.astype(BF16)
    r = x - hi.astype(F32)
    mid = r.astype(BF16)
    lo = (r - mid.astype(F32)).astype(BF16)
    return hi, mid, lo


def _foxprep_kernel(qk_ref, fl_ref, b_ref, o_ref, f_ref, *, rb):
    s_len = fl_ref.shape[0]
    x = fl_ref[...] + b_ref[...]
    logf = jnp.minimum(x, 0.0) - jnp.log(1.0 + jnp.exp(-jnp.abs(x)))
    hi, mid, lo = _split3(logf)
    for r in range(s_len // rb):
        row = lax.broadcasted_iota(jnp.int32, (rb, s_len), 0) + r * rb
        col = lax.broadcasted_iota(jnp.int32, (rb, s_len), 1)
        tri = jnp.where(col <= row, 1.0, 0.0).astype(BF16)
        f_ref[r * rb:(r + 1) * rb, :] = _dot(tri, hi) + _dot(tri, mid) + _dot(tri, lo)
    f = f_ref[...]
    lane = lax.broadcasted_iota(jnp.int32, (s_len, LANES), 1)
    for h in range(N_HEADS):
        fh = f[:, h:h + 1]
        fhi, fmid, flo = (t.astype(F32) for t in _split3(fh))
        qa = jnp.where(lane == 64, fhi, jnp.where(lane == 65, fmid, jnp.where(lane == 66, flo,
             jnp.where((lane >= 67) & (lane < 70), 1.0, 0.0))))
        ka = jnp.where(lane == 67, -fhi, jnp.where(lane == 68, -fmid, jnp.where(lane == 69, -flo,
             jnp.where((lane >= 64) & (lane < 67), 1.0, 0.0))))
        qs = slice(h * SLOT, (h + 1) * SLOT)
        ks = slice(N_HEADS * SLOT + h * SLOT, N_HEADS * SLOT + (h + 1) * SLOT)
        o_ref[:, qs] = (qk_ref[:, qs].astype(F32) + qa).astype(BF16)
        o_ref[:, ks] = (qk_ref[:, ks].astype(F32) + ka).astype(BF16)


def _foxprep(ob, of, bias, batch, s_len):
    n = ob.shape[0]
    w = 2 * N_HEADS * SLOT
    return pl.pallas_call(
        functools.partial(_foxprep_kernel, rb=256),
        grid=(batch,),
        in_specs=[
            pl.BlockSpec((s_len, w), lambda b: (b, 0)),
            pl.BlockSpec((s_len, LANES), lambda b: (b, OF_FL // LANES)),
            pl.BlockSpec((1, LANES), lambda b: (0, 0)),
        ],
        out_specs=pl.BlockSpec((s_len, w), lambda b: (b, 0)),
        out_shape=jax.ShapeDtypeStruct((n, w), BF16),
        scratch_shapes=[pltpu.VMEM((s_len, LANES), F32)],
        compiler_params=_cparams(("parallel",)),
        name="foxprep",
    )(ob, of, bias)


def _softmax_attn_kernel(q_ref, k_ref, v_ref, o_ref, *, tq, pair_sum):
    qi = pl.program_id(1)
    row = lax.broadcasted_iota(jnp.int32, (tq, tq), 0)
    col = lax.broadcasted_iota(jnp.int32, (tq, tq), 1)
    causal = col <= row
    slots = [slice(h * SLOT, (h + 1) * SLOT) for h in range(N_HEADS)]
    qs = [q_ref[:, hs] for hs in slots]

    ones = jnp.ones((tq, SLOT), BF16)

    def block(j, carry, masked):
        rows = pl.ds(pl.multiple_of(j * tq, tq), tq)
        ss = [_dot_nt(qs[h], k_ref[rows, slots[h]]) for h in range(N_HEADS)]
        ms, alphas, ps = [], [], []
        for h in range(N_HEADS):
            s = jnp.where(causal, ss[h], -1e30) if masked else ss[h]
            m_new = jnp.maximum(carry[h][0], jnp.max(s, axis=1, keepdims=True))
            ms.append(m_new)
            alphas.append(jnp.exp(carry[h][0] - m_new))
            ps.append(jnp.exp(s - m_new).astype(BF16))
        if pair_sum:
            pvs = []
            for pr in range(N_HEADS // 2):
                vx = jnp.concatenate([v_ref[rows, slots[pr]], ones], axis=1)
                both = _dot(jnp.concatenate([ps[2 * pr], ps[2 * pr + 1]], axis=0), vx)
                pvs += [both[:tq], both[tq:]]
        else:
            pvs = [_dot(ps[h], jnp.concatenate([v_ref[rows, slots[h]], ones], axis=1)) for h in range(N_HEADS)]
        return tuple((ms[h], alphas[h] * carry[h][1] + pvs[h][:, SLOT:], alphas[h] * carry[h][2] + pvs[h][:, :SLOT])
                     for h in range(N_HEADS))

    init = tuple((jnp.full((tq, 1), -1e30, F32), jnp.zeros((tq, SLOT), F32), jnp.zeros((tq, SLOT), F32))
                 for _ in range(N_HEADS))
    carry = lax.fori_loop(0, qi, lambda j, c: block(j, c, False), init)
    outs = [acc / l for _, l, acc in block(qi, carry, True)]
    if pair_sum:
        lane = lax.broadcasted_iota(jnp.int32, (tq, SLOT), 1)
        for pr in range(N_HEADS // 2):
            o_ref[:, slots[pr]] = jnp.where(lane < HEAD_DIM, outs[2 * pr], outs[2 * pr + 1])
    else:
        for h in range(N_HEADS):
            o_ref[:, slots[h]] = outs[h]


def _softmax_attn(q_arr, q_col, k_arr, k_col, v_arr, v_col, batch, s_len, pair_sum):
    n = q_arr.shape[0]
    tq = TQ_SOFTMAX
    nq = s_len // tq
    vw = GW // 2 if pair_sum else GW
    return pl.pallas_call(
        functools.partial(_softmax_attn_kernel, tq=tq, pair_sum=pair_sum),
        grid=(batch, nq),
        in_specs=[
            pl.BlockSpec((tq, GW), lambda b, i: (b * nq + i, q_col)),
            pl.BlockSpec((s_len, GW), lambda b, i: (b, k_col)),
            pl.BlockSpec((s_len, vw), lambda b, i: (b, v_col)),
        ],
        out_specs=pl.BlockSpec((tq, vw), lambda b, i: (b * nq + i, 0)),
        out_shape=jax.ShapeDtypeStruct((n, vw), F32),
        compiler_params=_cparams(("parallel", "arbitrary")),
        name="fox_attn" if pair_sum else "mla_attn",
    )(q_arr, k_arr, v_arr)


def _sb_attn_kernel(q_ref, k_ref, v_ref, o_ref, *, tq, tk):
    qi = pl.program_id(1)
    nz = tq // tk
    row = lax.broadcasted_iota(jnp.int32, (tq, tk), 0)
    col = lax.broadcasted_iota(jnp.int32, (tq, tk), 1)
    ur = lax.broadcasted_iota(jnp.int32, (tk, 2 * tk), 0)
    uc = lax.broadcasted_iota(jnp.int32, (tk, 2 * tk), 1)
    uo = jnp.where((uc >= tk) | (ur > uc), 1.0, 0.0).astype(BF16)
    slots = [slice(h * SLOT, (h + 1) * SLOT) for h in range(N_HEADS)]
    qs = [q_ref[:, hs] for hs in slots]

    lane = lax.broadcasted_iota(jnp.int32, (tq, SLOT), 1)

    def block(j, carry, zone):
        cs, accs = carry
        rows = pl.ds(pl.multiple_of(j * tk, tk), tk)
        zs = [_dot_nt(qs[h], k_ref[rows, slots[h]]) for h in range(N_HEADS)]
        strict = None if zone is None else (col + zone * tk) < row
        lbs, his, los = [], [], []
        for z in zs:
            lb = jnp.minimum(z, 0.0) - jnp.log(1.0 + jnp.exp(-jnp.abs(z)))
            lom = lb - z
            if strict is not None:
                lom = jnp.where(strict, lom, 0.0)
            hi = lom.astype(BF16)
            lbs.append(lb)
            his.append(hi)
            los.append((lom - hi.astype(F32)).astype(BF16))
        r = _dot(jnp.concatenate(his + los, axis=0), uo)
        ws, new_cs = [], []
        for h in range(N_HEADS):
            rh = r[h * tq:(h + 1) * tq] + r[(N_HEADS + h) * tq:(N_HEADS + h + 1) * tq]
            w = jnp.exp(lbs[h] + rh[:, :tk] + cs[h])
            if strict is not None:
                w = jnp.where(strict, w, 0.0)
            ws.append(w.astype(BF16))
            new_cs.append(cs[h] + rh[:, tk:])
        new_accs = []
        for pr in range(N_HEADS // 2):
            both = _dot(jnp.concatenate([ws[2 * pr], ws[2 * pr + 1]], axis=0), v_ref[rows, slots[pr]])
            new_accs.append(accs[pr] + jnp.where(lane < HEAD_DIM, both[:tq], both[tq:]))
        return tuple(new_cs), tuple(new_accs)

    carry = (tuple(jnp.zeros((tq, tk), F32) for _ in range(N_HEADS)),
             tuple(jnp.zeros((tq, SLOT), F32) for _ in range(N_HEADS // 2)))
    for zi in reversed(range(nz)):
        carry = block(qi * nz + zi, carry, zi)
    n_full = qi * nz
    carry = lax.fori_loop(0, n_full, lambda jj, cr: block(n_full - 1 - jj, cr, None), carry)
    for pr in range(N_HEADS // 2):
        o_ref[:, slots[pr]] = carry[1][pr]


def _sb_attn(ob, batch, s_len):
    n = ob.shape[0]
    tq, tk = TQ_SB, TK_SB
    nq = s_len // tq
    vw = GW // 2
    return pl.pallas_call(
        functools.partial(_sb_attn_kernel, tq=tq, tk=tk),
        grid=(batch, nq),
        in_specs=[
            pl.BlockSpec((tq, GW), lambda b, i: (b * nq + i, OB_SQ // GW)),
            pl.BlockSpec((s_len, GW), lambda b, i: (b, OB_SK // GW)),
            pl.BlockSpec((s_len, vw), lambda b, i: (b, OB_SV // vw)),
        ],
        out_specs=pl.BlockSpec((tq, vw), lambda b, i: (b * nq + i, 0)),
        out_shape=jax.ShapeDtypeStruct((n, vw), F32),
        compiler_params=_cparams(("parallel", "arbitrary")),
        name="sb_attn",
    )(ob, ob, ob)


def _mla_proj_kernel(cq_ref, ckv_ref, kra_ref, krb_ref, pos_ref, invf_ref, gq_ref, gkv_ref,
                     wqa_ref, wqb_ref, wk_ref, wv_ref, q_ref, k_ref, v_ref, *, scale):
    ang = pos_ref[...].astype(F32) * invf_ref[...]
    cs, sn = jnp.cos(ang), jnp.sin(ang)
    cqn = _rms(cq_ref[...], gq_ref[...]).astype(BF16)
    ckvn = _rms(ckv_ref[...], gkv_ref[...]).astype(BF16)
    qa, qb = _dot(cqn, wqa_ref[...]), _dot(cqn, wqb_ref[...])
    kn = _dot(ckvn, wk_ref[...])
    v_ref[...] = _dot(ckvn, wv_ref[...]).astype(BF16)
    kr = kra_ref[...] * cs + krb_ref[...] * sn
    for h in range(N_HEADS):
        hs = slice(h * SLOT, (h + 1) * SLOT)
        q_ref[:, hs] = ((qa[:, hs] * cs + qb[:, hs] * sn) * scale).astype(BF16)
        k_ref[:, hs] = (kn[:, hs] + kr).astype(BF16)


def _mla_proj(of, pos, invf, gq, gkv, wqa, wqb, wk, wv):
    n = of.shape[0]
    tm = TM_PROJ
    w = N_HEADS * SLOT
    const = lambda i: (0, 0)
    return pl.pallas_call(
        functools.partial(_mla_proj_kernel, scale=(MLA_NOPE + MLA_ROPE) ** -0.5),
        grid=(n // tm,),
        in_specs=[
            pl.BlockSpec((tm, MLA_Q_RANK), lambda i: (i, OF_CQ // MLA_Q_RANK)),
            pl.BlockSpec((tm, LANES), lambda i: (i, OF_CKV // LANES)),
            pl.BlockSpec((tm, LANES), lambda i: (i, OF_KRA // LANES)),
            pl.BlockSpec((tm, LANES), lambda i: (i, OF_KRB // LANES)),
            pl.BlockSpec((tm, 1), lambda i: (i, 0)),
            pl.BlockSpec((1, LANES), const),
            pl.BlockSpec((1, MLA_Q_RANK), const),
            pl.BlockSpec((1, MLA_KV_RANK), const),
            pl.BlockSpec((MLA_Q_RANK, w), const),
            pl.BlockSpec((MLA_Q_RANK, w), const),
            pl.BlockSpec((MLA_KV_RANK, w), const),
            pl.BlockSpec((MLA_KV_RANK, w), const),
        ],
        out_specs=[pl.BlockSpec((tm, w), lambda i: (i, 0))] * 3,
        out_shape=[jax.ShapeDtypeStruct((n, w), BF16)] * 3,
        compiler_params=_cparams(("parallel",)),
        name="mla_proj",
    )(of, of, of, of, pos, invf, gq, gkv, wqa, wqb, wk, wv)


def _outproj_kernel(yf_ref, ys_ref, ym_ref, g_ref, wo_ref, h_ref, o_ref):
    gw = yf_ref.shape[1]
    nf = _rms(yf_ref[...], g_ref[:, 0:gw]).astype(BF16)
    ns = _rms(ys_ref[...], g_ref[:, gw:2 * gw]).astype(BF16)
    nm = _rms(ym_ref[...], g_ref[:, 2 * gw:]).astype(BF16)
    acc = _dot(nf, wo_ref[0:gw, :]) + _dot(ns, wo_ref[gw:2 * gw, :]) + _dot(nm, wo_ref[2 * gw:, :])
    o_ref[...] = h_ref[...] + acc


def _outproj(yf, ys, ym, g, wo, h):
    n = h.shape[0]
    tm = TM_PROJ
    const = lambda i: (0, 0)
    return pl.pallas_call(
        _outproj_kernel,
        grid=(n // tm,),
        in_specs=[
            pl.BlockSpec((tm, yf.shape[1]), lambda i: (i, 0)),
            pl.BlockSpec((tm, ys.shape[1]), lambda i: (i, 0)),
            pl.BlockSpec((tm, ym.shape[1]), lambda i: (i, 0)),
            pl.BlockSpec((1, D_MODEL), const),
            pl.BlockSpec((D_MODEL, D_MODEL), const),
            pl.BlockSpec((tm, D_MODEL), lambda i: (i, 0)),
        ],
        out_specs=pl.BlockSpec((tm, D_MODEL), lambda i: (i, 0)),
        out_shape=jax.ShapeDtypeStruct((n, D_MODEL), F32),
        compiler_params=_cparams(("parallel",)),
        name="outproj",
    )(yf, ys, ym, g, wo, h)


def _top_rows(e, n):
    rows = []
    for r in range(n):
        m = jnp.max(e, axis=0, keepdims=True)
        rows.append(m)
        if r + 1 < n:
            e = jnp.where(e == m, -1.0, e)
    return rows


def _peer_kernel(*refs, tile, eb, ns):
    (h_ref, g_ref, wqh_ref, wql_ref, sk_ref, u_ref, vt_ref,
     o_ref, xnt_ref, e2_ref, e1s_ref, g16_ref, tv_ref, cand_ref, candn_ref,
     pre0_ref, pre1_ref, act0_ref, act1_ref, acc_ref) = refs
    s = pl.program_id(1)

    @pl.when(s == 0)
    def _():
        xn = _rms(h_ref[...], g_ref[...])
        xnt = xn.T
        x_hi = xnt.astype(BF16)
        x_lo = (xnt - x_hi.astype(F32)).astype(BF16)
        xnt_ref[...] = x_hi
        qt = _dot(wqh_ref[...], x_hi) + (_dot(wqh_ref[...], x_lo) + _dot(wql_ref[...], x_hi))
        cand_ref[N_CAND:, :] = jnp.full((N_CAND_PAD - N_CAND, tile), -1.0, F32)
        candn_ref[N_CAND:, :] = jnp.full((N_CAND_PAD - N_CAND, tile), -1.0, F32)
        for hd in range(PEER_HEADS):
            es = []
            for p in range(2):
                base = (hd * 2 + p) * PEER_HALF
                sc = _dot_hi(sk_ref[hd * 2 + p], qt[base:base + PEER_HALF, :])
                e = jnp.exp(sc - jnp.max(sc, axis=0, keepdims=True))
                es.append(e)
                for r, m in enumerate(_top_rows(e, PEER_TOPK)):
                    tv_ref[p, r:r + 1, :] = m
            for i, (a, b) in enumerate(PEER_PAIRS):
                cand_ref[i:i + 1, :] = tv_ref[0, a:a + 1, :] * tv_ref[1, b:b + 1, :]
            cand = cand_ref[...]
            tau = _top_rows(cand, PEER_TOPK)[-1]
            sel = cand >= tau
            inv_z = 1.0 / jnp.sum(jnp.where(sel, cand, 0.0), axis=0, keepdims=True)
            e1s_ref[hd] = es[0] * inv_z
            for lg in range(tile // LANES):
                e2_ref[hd, lg] = es[1][:, lg * LANES:(lg + 1) * LANES]
            t1s = tv_ref[0] * inv_z
            for i, (a, b) in enumerate(PEER_PAIRS):
                candn_ref[i:i + 1, :] = t1s[a:a + 1, :] * tv_ref[1, b:b + 1, :]
            g16_ref[hd] = jnp.min(jnp.where(sel, candn_ref[...], jnp.inf), axis=0, keepdims=True)
        for ref in (pre0_ref, pre1_ref, act0_ref, act1_ref, acc_ref):
            ref[...] = jnp.zeros_like(ref)

    npc = eb // PEER_NKEYS
    nlg = tile // LANES
    n1_base = jnp.clip(s - 1, 0, ns - 1) * npc
    pre_refs, act_refs = (pre0_ref, pre1_ref), (act0_ref, act1_ref)

    def stage_act(a, prv, lane_groups):
        rc = PEER_NKEYS // 2
        e1full = [e1s_ref[hd, pl.ds(n1_base + a, 1), :] for hd in range(PEER_HEADS)]
        for lg in lane_groups:
            cols = slice(lg * LANES, (lg + 1) * LANES)
            e1rows = [row[:, cols] for row in e1full]
            g16rows = [g16_ref[hd, :, cols] for hd in range(PEER_HEADS)]
            for r0 in range(0, PEER_NKEYS, rc):
                gate = None
                for hd in range(PEER_HEADS):
                    val = e2_ref[hd, lg, r0:r0 + rc, :] * e1rows[hd]
                    contrib = jnp.where(val >= g16rows[hd], val, 0.0)
                    gate = contrib if gate is None else gate + contrib
                rows = slice(a * PEER_NKEYS + r0, a * PEER_NKEYS + r0 + rc)
                pa = pre_refs[prv][lg, rows, :]
                gelu = 0.5 * pa * (1.0 + lax.erf(pa * (1.0 / math.sqrt(2.0))))
                act_refs[prv][lg, rows, :] = (gate * gelu).astype(BF16)

    def step(cur):
        prv = 1 - cur
        pre_k, out_k = D_MODEL // MXU_PIECES, eb // MXU_PIECES

        def pre_piece(k):
            ks = slice(k * pre_k, (k + 1) * pre_k)
            part = _dot(u_ref[:, ks], xnt_ref[ks, :])
            for lg in range(nlg):
                cols = slice(lg * LANES, (lg + 1) * LANES)
                if k == 0:
                    pre_refs[cur][lg] = part[:, cols]
                else:
                    pre_refs[cur][lg] += part[:, cols]

        def out_piece(k):
            ks = slice(k * out_k, (k + 1) * out_k)
            act = jnp.concatenate([act_refs[cur][lg, ks, :] for lg in range(nlg)], axis=1)
            acc_ref[...] += _dot(vt_ref[:, ks], act)

        mxu = [functools.partial(fn, k) for k in range(MXU_PIECES) for fn in (pre_piece, out_piece)]
        vpu = [functools.partial(stage_act, a, prv, (lg,)) for a in range(npc) for lg in range(tile // LANES)]
        done = 0
        for k, piece in enumerate(mxu):
            piece()
            upto = (k + 1) * len(vpu) // len(mxu)
            for fn in vpu[done:upto]:
                fn()
            done = upto

    for parity in range(2):
        pl.when(s % 2 == parity)(functools.partial(step, parity))

    @pl.when(s == pl.num_programs(1) - 1)
    def _():
        o_ref[...] = h_ref[...] + acc_ref[...].T


def _peer(h, g, wqt_hi, wqt_lo, sk, u, v):
    n = h.shape[0]
    n_exp = u.shape[0]
    tile, eb = T_PEER, EB_PEER
    ns = n_exp // eb
    dq = wqt_hi.shape[0]
    vt = v.reshape(ns, eb, D_MODEL).transpose(0, 2, 1).astype(BF16)
    u_map = lambda i, s: (jnp.minimum(s, ns - 1), 0)
    vt_map = lambda i, s: (jnp.clip(s - 2, 0, ns - 1), 0, 0)
    return pl.pallas_call(
        functools.partial(_peer_kernel, tile=tile, eb=eb, ns=ns),
        grid=(n // tile, ns + 2),
        in_specs=[
            pl.BlockSpec((tile, D_MODEL), lambda i, s: (i, 0)),
            pl.BlockSpec((1, D_MODEL), lambda i, s: (0, 0)),
            pl.BlockSpec((dq, D_MODEL), lambda i, s: (0, 0)),
            pl.BlockSpec((dq, D_MODEL), lambda i, s: (0, 0)),
            pl.BlockSpec((2 * PEER_HEADS, PEER_NKEYS, PEER_HALF), lambda i, s: (0, 0, 0)),
            pl.BlockSpec((eb, D_MODEL), u_map),
            pl.BlockSpec((None, D_MODEL, eb), vt_map),
        ],
        out_specs=pl.BlockSpec((tile, D_MODEL), lambda i, s: (i, 0)),
        out_shape=jax.ShapeDtypeStruct((n, D_MODEL), F32),
        scratch_shapes=[
            pltpu.VMEM((D_MODEL, tile), BF16),
            pltpu.VMEM((PEER_HEADS, tile // LANES, PEER_NKEYS, LANES), F32),
            pltpu.VMEM((PEER_HEADS, PEER_NKEYS, tile), F32),
            pltpu.VMEM((PEER_HEADS, 1, tile), F32),
            pltpu.VMEM((2, PEER_TOPK, tile), F32),
            pltpu.VMEM((N_CAND_PAD, tile), F32),
            pltpu.VMEM((N_CAND_PAD, tile), F32),
            pltpu.VMEM((tile // LANES, eb, LANES), F32),
            pltpu.VMEM((tile // LANES, eb, LANES), F32),
            pltpu.VMEM((tile // LANES, eb, LANES), BF16),
            pltpu.VMEM((tile // LANES, eb, LANES), BF16),
            pltpu.VMEM((D_MODEL, tile), F32),
        ],
        compiler_params=_cparams(("parallel", "arbitrary")),
        name="peer",
    )(h, g, wqt_hi, wqt_lo, sk, u, vt)


def _ple_kernel(h_ref, p_ref, g_ref, wg_ref, wp_ref, fg_ref, o_ref, *, final):
    h = h_ref[...]
    gate = jax.nn.sigmoid(_dot(_rms(h, g_ref[...]).astype(BF16), wg_ref[...]))
    out = h + gate * _dot(p_ref[...].astype(BF16), wp_ref[...])
    o_ref[...] = _rms(out, fg_ref[...]) if final else out


def _ple(h, p, g, wg, wp, fg, final):
    n = h.shape[0]
    tm = TM_PROJ
    const = lambda i: (0, 0)
    return pl.pallas_call(
        functools.partial(_ple_kernel, final=final),
        grid=(n // tm,),
        in_specs=[
            pl.BlockSpec((tm, D_MODEL), lambda i: (i, 0)),
            pl.BlockSpec((tm, PLE_DIM), lambda i: (i, 0)),
            pl.BlockSpec((1, D_MODEL), const),
            pl.BlockSpec((D_MODEL, D_MODEL), const),
            pl.BlockSpec((PLE_DIM, D_MODEL), const),
            pl.BlockSpec((1, D_MODEL), const),
        ],
        out_specs=pl.BlockSpec((tm, D_MODEL), lambda i: (i, 0)),
        out_shape=jax.ShapeDtypeStruct((n, D_MODEL), F32),
        compiler_params=_cparams(("parallel",)),
        name="ple",
    )(h, p, g, wg, wp, fg)


def _pad_heads(w, width=SLOT):
    k = w.shape[0]
    w = w.reshape(k, N_HEADS, -1)
    return jnp.pad(w, ((0, 0), (0, 0), (0, width - w.shape[2]))).reshape(k, N_HEADS * width)


def _rot_partner(w_rot):
    half = MLA_ROPE // 2
    return jnp.concatenate([-w_rot[..., half:], w_rot[..., :half]], axis=-1)


def _layer_weights(w_in, w_uq, w_ukv):
    offs = [int(o) for o in np.cumsum(IN_SPLITS)[:-1]]
    fq, fk, fv, fl, sq, sk, sv, cq, ckv, kr = jnp.split(w_in, offs, axis=1)
    att_scale = HEAD_DIM ** -0.5
    wb = jnp.concatenate([_pad_heads(fq * att_scale), _pad_heads(fk),
                          _pad_heads(sq * att_scale), _pad_heads(sk), fv, sv], axis=1).astype(BF16)
    z = lambda c: jnp.zeros((D_MODEL, c), F32)
    kra = jnp.concatenate([z(MLA_NOPE), kr, z(SLOT - MLA_NOPE - MLA_ROPE)], axis=1)
    krb = jnp.concatenate([z(MLA_NOPE), _rot_partner(kr), z(SLOT - MLA_NOPE - MLA_ROPE)], axis=1)
    wf = jnp.concatenate([cq, ckv, kra, krb, fl, z(LANES - N_HEADS)], axis=1).astype(BF16)
    uq = w_uq.reshape(MLA_Q_RANK, N_HEADS, MLA_NOPE + MLA_ROPE)
    zq = jnp.zeros((MLA_Q_RANK, N_HEADS, SLOT - MLA_NOPE - MLA_ROPE), F32)
    wqa = jnp.concatenate([uq, zq], axis=2).reshape(MLA_Q_RANK, N_HEADS * SLOT).astype(BF16)
    wqb = jnp.concatenate([jnp.zeros_like(uq[..., :MLA_NOPE]), _rot_partner(uq[..., MLA_NOPE:]), zq],
                          axis=2).reshape(MLA_Q_RANK, N_HEADS * SLOT).astype(BF16)
    ukv = w_ukv.reshape(MLA_KV_RANK, N_HEADS, MLA_NOPE + MLA_V)
    wk = jnp.pad(ukv[..., :MLA_NOPE], ((0, 0), (0, 0), (0, SLOT - MLA_NOPE))).reshape(MLA_KV_RANK, -1).astype(BF16)
    wv = ukv[..., MLA_NOPE:].reshape(MLA_KV_RANK, -1).astype(BF16)
    return wb, wf, wqa, wqb, wk, wv


def _rope_lane_freqs():
    half = MLA_ROPE // 2
    inv_freq = ROPE_THETA ** (-jnp.arange(half, dtype=F32) / half)
    zeros = lambda c: jnp.zeros((c,), F32)
    return jnp.concatenate([zeros(MLA_NOPE), inv_freq, inv_freq, zeros(SLOT - MLA_NOPE - MLA_ROPE)]).reshape(1, SLOT)


def kernel(x, p, positions, norm_mix_g, w_in, b_forget, mla_q_norm_g, w_uq, mla_kv_norm_g, w_ukv, mix_out_norm_g, w_o, norm_ffn_g, peer_w_query, peer_sub_keys, peer_u, peer_v, w_ple, ple_norm_g, w_ple_gate, final_norm_g):
    batch, s_len, d = x.shape
    depth = p.shape[0]
    n = batch * s_len
    h = x.reshape(n, d)
    pos = positions.reshape(n, 1)
    invf = _rope_lane_freqs()
    row = lambda v: v.reshape(1, -1)
    for i in range(depth):
        wb, wf, wqa, wqb, wk, wv = _layer_weights(w_in[i], w_uq[i], w_ukv[i])
        ob, of = _inproj(h, row(norm_mix_g[i]), wb, wf)
        bias = jnp.pad(b_forget[i], (0, LANES - N_HEADS)).reshape(1, LANES)
        fqk = _foxprep(ob, of, bias, batch, s_len)
        y_fox = _softmax_attn(fqk, 0, fqk, 1, ob, OB_FV // (GW // 2), batch, s_len, True)
        y_sb = _sb_attn(ob, batch, s_len)
        mq, mk, mv = _mla_proj(of, pos, invf, row(mla_q_norm_g[i]), row(mla_kv_norm_g[i]), wqa, wqb, wk, wv)
        y_mla = _softmax_attn(mq, 0, mk, 0, mv, 0, batch, s_len, False)
        h = _outproj(y_fox, y_sb, y_mla, row(mix_out_norm_g[i]), w_o[i].astype(BF16), h)
        sk = peer_sub_keys[i].reshape(2 * PEER_HEADS, PEER_NKEYS, PEER_HALF)
        wqt = peer_w_query[i].T
        wqt_hi = wqt.astype(BF16)
        wqt_lo = (wqt - wqt_hi.astype(F32)).astype(BF16)
        h = _peer(h, row(norm_ffn_g[i]), wqt_hi, wqt_lo, sk, peer_u[i].astype(BF16), peer_v[i])
        h = _ple(h, p[i].reshape(n, PLE_DIM), row(ple_norm_g[i]), w_ple_gate[i].astype(BF16),
                 w_ple[i].astype(BF16), row(final_norm_g), i == depth - 1)
    return h.reshape(batch, s_len, d)
```

```python
import functools
import math

import numpy as np
import jax
import jax.numpy as jnp
from jax import lax
from jax.experimental import pallas as pl
from jax.experimental.pallas import tpu as pltpu

F32 = jnp.float32
BF16 = jnp.bfloat16

D_MODEL = 1024
HEAD_DIM = 64
N_HEADS = 4
MLA_Q_RANK = 256
MLA_KV_RANK = 128
MLA_NOPE = 64
MLA_ROPE = 32
MLA_V = 128
ROPE_THETA = 10000.0
IN_SPLITS = (256, 256, 256, 4, 256, 256, 256, MLA_Q_RANK, MLA_KV_RANK, MLA_ROPE)
PEER_HEADS = 8
PEER_NKEYS = 128
PEER_HALF = 64
PEER_TOPK = 16
PLE_DIM = 256
EPS = 1e-6

LANES = 128
SLOT = LANES
VMEM_LIMIT = 56 * 1024 * 1024

TM_PROJ = 512
TQ_SOFTMAX = 512
TQ_SB = 256
TK_SB = 128
T_PEER = 512
EB_PEER = 1024

OB_FQ, OB_FK, OB_SQ, OB_SK, OB_FV, OB_SV, OB_W = 0, 512, 1024, 1536, 2048, 2304, 2560
GW = N_HEADS * SLOT
OF_CQ, OF_CKV, OF_KRA, OF_KRB, OF_FL, OF_W = 0, 256, 384, 512, 640, 768

PEER_PAIRS = tuple((a, b) for a in range(PEER_TOPK) for b in range(PEER_TOPK) if (a + 1) * (b + 1) <= PEER_TOPK)
N_CAND = len(PEER_PAIRS)
N_CAND_PAD = 56


def _cparams(sem):
    return pltpu.CompilerParams(dimension_semantics=sem, vmem_limit_bytes=VMEM_LIMIT)


def _rms(x, g):
    return x * lax.rsqrt(jnp.mean(x * x, axis=-1, keepdims=True) + EPS) * g


def _dot(a, b):
    return jnp.dot(a, b, preferred_element_type=F32)


def _dot_nt(a, b):
    return lax.dot_general(a, b, (((1,), (1,)), ((), ())), preferred_element_type=F32)


def _dot_hi(a, b):
    return jnp.dot(a, b, preferred_element_type=F32, precision=lax.Precision.HIGHEST)


def _inproj_kernel(x_ref, g_ref, wb_ref, wf_ref, ob_ref, of_ref):
    xb = _rms(x_ref[...], g_ref[...]).astype(BF16)
    ob_ref[...] = _dot(xb, wb_ref[...]).astype(BF16)
    of_ref[...] = _dot(xb, wf_ref[...])


def _inproj(h, g, wb, wf):
    n = h.shape[0]
    tm = TM_PROJ
    return pl.pallas_call(
        _inproj_kernel,
        grid=(n // tm,),
        in_specs=[
            pl.BlockSpec((tm, D_MODEL), lambda i: (i, 0)),
            pl.BlockSpec((1, D_MODEL), lambda i: (0, 0)),
            pl.BlockSpec((D_MODEL, OB_W), lambda i: (0, 0)),
            pl.BlockSpec((D_MODEL, OF_W), lambda i: (0, 0)),
        ],
        out_specs=[
            pl.BlockSpec((tm, OB_W), lambda i: (i, 0)),
            pl.BlockSpec((tm, OF_W), lambda i: (i, 0)),
        ],
        out_shape=[jax.ShapeDtypeStruct((n, OB_W), BF16), jax.ShapeDtypeStruct((n, OF_W), F32)],
        compiler_params=_cparams(("parallel",)),
        name="inproj",
    )(h, g, wb, wf)


def _split3(x):
    hi = x.astype(BF16)
    r = x - hi.astype(F32)
    mid = r.astype(BF16)
    lo = (r - mid.astype(F32)).astype(BF16)
    return hi, mid, lo


def _foxprep_kernel(qk_ref, fl_ref, b_ref, o_ref, f_ref, *, rb):
    s_len = fl_ref.shape[0]
    x = fl_ref[...] + b_ref[...]
    logf = jnp.minimum(x, 0.0) - jnp.log(1.0 + jnp.exp(-jnp.abs(x)))
    hi, mid, lo = _split3(logf)
    for r in range(s_len // rb):
        row = lax.broadcasted_iota(jnp.int32, (rb, s_len), 0) + r * rb
        col = lax.broadcasted_iota(jnp.int32, (rb, s_len), 1)
        tri = jnp.where(col <= row, 1.0, 0.0).astype(BF16)
        f_ref[r * rb:(r + 1) * rb, :] = _dot(tri, hi) + _dot(tri, mid) + _dot(tri, lo)
    f = f_ref[...]
    lane = lax.broadcasted_iota(jnp.int32, (s_len, LANES), 1)
    for h in range(N_HEADS):
        fh = f[:, h:h + 1]
        fhi, fmid, flo = (t.astype(F32) for t in _split3(fh))
        qa = jnp.where(lane == 64, fhi, jnp.where(lane == 65, fmid, jnp.where(lane == 66, flo,
             jnp.where((lane >= 67) & (lane < 70), 1.0, 0.0))))
        ka = jnp.where(lane == 67, -fhi, jnp.where(lane == 68, -fmid, jnp.where(lane == 69, -flo,
             jnp.where((lane >= 64) & (lane < 67), 1.0, 0.0))))
        qs = slice(h * SLOT, (h + 1) * SLOT)
        ks = slice(N_HEADS * SLOT + h * SLOT, N_HEADS * SLOT + (h + 1) * SLOT)
        o_ref[:, qs] = (qk_ref[:, qs].astype(F32) + qa).astype(BF16)
        o_ref[:, ks] = (qk_ref[:, ks].astype(F32) + ka).astype(BF16)


def _foxprep(ob, of, bias, batch, s_len):
    n = ob.shape[0]
    w = 2 * N_HEADS * SLOT
    return pl.pallas_call(
        functools.partial(_foxprep_kernel, rb=256),
        grid=(batch,),
        in_specs=[
            pl.BlockSpec((s_len, w), lambda b: (b, 0)),
            pl.BlockSpec((s_len, LANES), lambda b: (b, OF_FL // LANES)),
            pl.BlockSpec((1, LANES), lambda b: (0, 0)),
        ],
        out_specs=pl.BlockSpec((s_len, w), lambda b: (b, 0)),
        out_shape=jax.ShapeDtypeStruct((n, w), BF16),
        scratch_shapes=[pltpu.VMEM((s_len, LANES), F32)],
        compiler_params=_cparams(("parallel",)),
        name="foxprep",
    )(ob, of, bias)


def _softmax_attn_kernel(q_ref, k_ref, v_ref, o_ref, *, tq, pair_sum):
    qi = pl.program_id(1)
    row = lax.broadcasted_iota(jnp.int32, (tq, tq), 0)
    col = lax.broadcasted_iota(jnp.int32, (tq, tq), 1)
    causal = col <= row
    slots = [slice(h * SLOT, (h + 1) * SLOT) for h in range(N_HEADS)]
    qs = [q_ref[:, hs] for hs in slots]

    ones = jnp.ones((tq, SLOT), BF16)

    def block(j, carry, masked):
        rows = pl.ds(pl.multiple_of(j * tq, tq), tq)
        ss = [_dot_nt(qs[h], k_ref[rows, slots[h]]) for h in range(N_HEADS)]
        ms, alphas, ps = [], [], []
        for h in range(N_HEADS):
            s = jnp.where(causal, ss[h], -1e30) if masked else ss[h]
            m_new = jnp.maximum(carry[h][0], jnp.max(s, axis=1, keepdims=True))
            ms.append(m_new)
            alphas.append(jnp.exp(carry[h][0] - m_new))
            ps.append(jnp.exp(s - m_new).astype(BF16))
        if pair_sum:
            pvs = []
            for pr in range(N_HEADS // 2):
                vx = jnp.concatenate([v_ref[rows, slots[pr]], ones], axis=1)
                both = _dot(jnp.concatenate([ps[2 * pr], ps[2 * pr + 1]], axis=0), vx)
                pvs += [both[:tq], both[tq:]]
        else:
            pvs = [_dot(ps[h], jnp.concatenate([v_ref[rows, slots[h]], ones], axis=1)) for h in range(N_HEADS)]
        return tuple((ms[h], alphas[h] * carry[h][1] + pvs[h][:, SLOT:], alphas[h] * carry[h][2] + pvs[h][:, :SLOT])
                     for h in range(N_HEADS))

    init = tuple((jnp.full((tq, 1), -1e30, F32), jnp.zeros((tq, SLOT), F32), jnp.zeros((tq, SLOT), F32))
                 for _ in range(N_HEADS))
    carry = lax.fori_loop(0, qi, lambda j, c: block(j, c, False), init)
    outs = [acc / l for _, l, acc in block(qi, carry, True)]
    if pair_sum:
        lane = lax.broadcasted_iota(jnp.int32, (tq, SLOT), 1)
        for pr in range(N_HEADS // 2):
            o_ref[:, slots[pr]] = jnp.where(lane < HEAD_DIM, outs[2 * pr], outs[2 * pr + 1])
    else:
        for h in range(N_HEADS):
            o_ref[:, slots[h]] = outs[h]


def _softmax_attn(q_arr, q_col, k_arr, k_col, v_arr, v_col, batch, s_len, pair_sum):
    n = q_arr.shape[0]
    tq = TQ_SOFTMAX
    nq = s_len // tq
    vw = GW // 2 if pair_sum else GW
    return pl.pallas_call(
        functools.partial(_softmax_attn_kernel, tq=tq, pair_sum=pair_sum),
        grid=(batch, nq),
        in_specs=[
            pl.BlockSpec((tq, GW), lambda b, i: (b * nq + i, q_col)),
            pl.BlockSpec((s_len, GW), lambda b, i: (b, k_col)),
            pl.BlockSpec((s_len, vw), lambda b, i: (b, v_col)),
        ],
        out_specs=pl.BlockSpec((tq, vw), lambda b, i: (b * nq + i, 0)),
        out_shape=jax.ShapeDtypeStruct((n, vw), F32),
        compiler_params=_cparams(("parallel", "arbitrary")),
        name="fox_attn" if pair_sum else "mla_attn",
    )(q_arr, k_arr, v_arr)


def _sb_attn_kernel(q_ref, k_ref, v_ref, o_ref, *, tq, tk):
    qi = pl.program_id(1)
    nz = tq // tk
    row = lax.broadcasted_iota(jnp.int32, (tq, tk), 0)
    col = lax.broadcasted_iota(jnp.int32, (tq, tk), 1)
    ur = lax.broadcasted_iota(jnp.int32, (tk, 2 * tk), 0)
    uc = lax.broadcasted_iota(jnp.int32, (tk, 2 * tk), 1)
    uo = jnp.where((uc >= tk) | (ur > uc), 1.0, 0.0).astype(BF16)
    uo2 = jnp.concatenate([uo, uo], axis=0)
    slots = [slice(h * SLOT, (h + 1) * SLOT) for h in range(N_HEADS)]
    qs = [q_ref[:, hs] for hs in slots]

    lane = lax.broadcasted_iota(jnp.int32, (tq, SLOT), 1)

    def blocks(js, carry, zones):
        cs, accs = list(carry[0]), list(carry[1])
        rows = [pl.ds(pl.multiple_of(j * tk, tk), tk) for j in js]
        stricts = [None if zn is None else (col + zn * tk) < row for zn in zones]
        lbs, hls = [], []
        for b in range(len(js)):
            for h in range(N_HEADS):
                z = _dot_nt(qs[h], k_ref[rows[b], slots[h]])
                lb = jnp.minimum(z, 0.0) - jnp.log(1.0 + jnp.exp(-jnp.abs(z)))
                lom = lb - z
                if stricts[b] is not None:
                    lom = jnp.where(stricts[b], lom, 0.0)
                hi = lom.astype(BF16)
                lo = (lom - hi.astype(F32)).astype(BF16)
                lbs.append(lb)
                hls.append(jnp.concatenate([hi, lo], axis=1))
        r = _dot(jnp.concatenate(hls, axis=0), uo2)
        for b in range(len(js)):
            ws = []
            for h in range(N_HEADS):
                rh = r[(b * N_HEADS + h) * tq:(b * N_HEADS + h + 1) * tq]
                w = jnp.exp(lbs[b * N_HEADS + h] + rh[:, :tk] + cs[h])
                if stricts[b] is not None:
                    w = jnp.where(stricts[b], w, 0.0)
                ws.append(w.astype(BF16))
                cs[h] = cs[h] + rh[:, tk:]
            for pr in range(N_HEADS // 2):
                both = _dot(jnp.concatenate([ws[2 * pr], ws[2 * pr + 1]], axis=0), v_ref[rows[b], slots[pr]])
                accs[pr] = accs[pr] + jnp.where(lane < HEAD_DIM, both[:tq], both[tq:])
        return tuple(cs), tuple(accs)

    carry = (tuple(jnp.zeros((tq, tk), F32) for _ in range(N_HEADS)),
             tuple(jnp.zeros((tq, SLOT), F32) for _ in range(N_HEADS // 2)))
    zone = list(reversed(range(nz)))
    carry = blocks([qi * nz + zi for zi in zone], carry, zone)
    n_full = qi * nz
    carry = lax.fori_loop(
        0, qi, lambda jj, cr: blocks([n_full - 1 - jj * nz - i for i in range(nz)], cr, [None] * nz), carry)
    for pr in range(N_HEADS // 2):
        o_ref[:, slots[pr]] = carry[1][pr]


def _sb_attn(ob, batch, s_len):
    n = ob.shape[0]
    tq, tk = TQ_SB, TK_SB
    nq = s_len // tq
    vw = GW // 2
    return pl.pallas_call(
        functools.partial(_sb_attn_kernel, tq=tq, tk=tk),
        grid=(batch, nq),
        in_specs=[
            pl.BlockSpec((tq, GW), lambda b, i: (b * nq + i, OB_SQ // GW)),
            pl.BlockSpec((s_len, GW), lambda b, i: (b, OB_SK // GW)),
            pl.BlockSpec((s_len, vw), lambda b, i: (b, OB_SV // vw)),
        ],
        out_specs=pl.BlockSpec((tq, vw), lambda b, i: (b * nq + i, 0)),
        out_shape=jax.ShapeDtypeStruct((n, vw), F32),
        compiler_params=_cparams(("parallel", "arbitrary")),
        name="sb_attn",
    )(ob, ob, ob)


def _mla_proj_kernel(cq_ref, ckv_ref, kra_ref, krb_ref, pos_ref, invf_ref, gq_ref, gkv_ref,
                     wqa_ref, wqb_ref, wk_ref, wv_ref, q_ref, k_ref, v_ref, *, scale):
    ang = pos_ref[...].astype(F32) * invf_ref[...]
    cs, sn = jnp.cos(ang), jnp.sin(ang)
    cqn = _rms(cq_ref[...], gq_ref[...]).astype(BF16)
    ckvn = _rms(ckv_ref[...], gkv_ref[...]).astype(BF16)
    qa, qb = _dot(cqn, wqa_ref[...]), _dot(cqn, wqb_ref[...])
    kn = _dot(ckvn, wk_ref[...])
    v_ref[...] = _dot(ckvn, wv_ref[...]).astype(BF16)
    kr = kra_ref[...] * cs + krb_ref[...] * sn
    for h in range(N_HEADS):
        hs = slice(h * SLOT, (h + 1) * SLOT)
        q_ref[:, hs] = ((qa[:, hs] * cs + qb[:, hs] * sn) * scale).astype(BF16)
        k_ref[:, hs] = (kn[:, hs] + kr).astype(BF16)


def _mla_proj(of, pos, invf, gq, gkv, wqa, wqb, wk, wv):
    n = of.shape[0]
    tm = TM_PROJ
    w = N_HEADS * SLOT
    const = lambda i: (0, 0)
    return pl.pallas_call(
        functools.partial(_mla_proj_kernel, scale=(MLA_NOPE + MLA_ROPE) ** -0.5),
        grid=(n // tm,),
        in_specs=[
            pl.BlockSpec((tm, MLA_Q_RANK), lambda i: (i, OF_CQ // MLA_Q_RANK)),
            pl.BlockSpec((tm, LANES), lambda i: (i, OF_CKV // LANES)),
            pl.BlockSpec((tm, LANES), lambda i: (i, OF_KRA // LANES)),
            pl.BlockSpec((tm, LANES), lambda i: (i, OF_KRB // LANES)),
            pl.BlockSpec((tm, 1), lambda i: (i, 0)),
            pl.BlockSpec((1, LANES), const),
            pl.BlockSpec((1, MLA_Q_RANK), const),
            pl.BlockSpec((1, MLA_KV_RANK), const),
            pl.BlockSpec((MLA_Q_RANK, w), const),
            pl.BlockSpec((MLA_Q_RANK, w), const),
            pl.BlockSpec((MLA_KV_RANK, w), const),
            pl.BlockSpec((MLA_KV_RANK, w), const),
        ],
        out_specs=[pl.BlockSpec((tm, w), lambda i: (i, 0))] * 3,
        out_shape=[jax.ShapeDtypeStruct((n, w), BF16)] * 3,
        compiler_params=_cparams(("parallel",)),
        name="mla_proj",
    )(of, of, of, of, pos, invf, gq, gkv, wqa, wqb, wk, wv)


def _outproj_kernel(yf_ref, ys_ref, ym_ref, g_ref, wo_ref, h_ref, o_ref):
    gw = yf_ref.shape[1]
    nf = _rms(yf_ref[...], g_ref[:, 0:gw]).astype(BF16)
    ns = _rms(ys_ref[...], g_ref[:, gw:2 * gw]).astype(BF16)
    nm = _rms(ym_ref[...], g_ref[:, 2 * gw:]).astype(BF16)
    acc = _dot(nf, wo_ref[0:gw, :]) + _dot(ns, wo_ref[gw:2 * gw, :]) + _dot(nm, wo_ref[2 * gw:, :])
    o_ref[...] = h_ref[...] + acc


def _outproj(yf, ys, ym, g, wo, h):
    n = h.shape[0]
    tm = TM_PROJ
    const = lambda i: (0, 0)
    return pl.pallas_call(
        _outproj_kernel,
        grid=(n // tm,),
        in_specs=[
            pl.BlockSpec((tm, yf.shape[1]), lambda i: (i, 0)),
            pl.BlockSpec((tm, ys.shape[1]), lambda i: (i, 0)),
            pl.BlockSpec((tm, ym.shape[1]), lambda i: (i, 0)),
            pl.BlockSpec((1, D_MODEL), const),
            pl.BlockSpec((D_MODEL, D_MODEL), const),
            pl.BlockSpec((tm, D_MODEL), lambda i: (i, 0)),
        ],
        out_specs=pl.BlockSpec((tm, D_MODEL), lambda i: (i, 0)),
        out_shape=jax.ShapeDtypeStruct((n, D_MODEL), F32),
        compiler_params=_cparams(("parallel",)),
        name="outproj",
    )(yf, ys, ym, g, wo, h)


def _top_rows(e, n):
    rows = []
    for r in range(n):
        m = jnp.max(e, axis=0, keepdims=True)
        rows.append(m)
        if r + 1 < n:
            e = jnp.where(e == m, -1.0, e)
    return rows


def _peer_kernel(*refs, tile, eb):
    (h_ref, g_ref, wqh_ref, wql_ref, sk_ref, u_ref, vt_ref,
     o_ref, xnt_ref, e2_ref, e1s_ref, g16_ref, tv_ref, cand_ref, candn_ref,
     pre_ref, act_ref, acc_ref) = refs
    s = pl.program_id(1)

    @pl.when(s == 0)
    def _():
        xn = _rms(h_ref[...], g_ref[...])
        xnt = xn.T
        x_hi = xnt.astype(BF16)
        x_lo = (xnt - x_hi.astype(F32)).astype(BF16)
        xnt_ref[...] = x_hi
        qt = _dot(wqh_ref[...], x_hi) + (_dot(wqh_ref[...], x_lo) + _dot(wql_ref[...], x_hi))
        cand_ref[N_CAND:, :] = jnp.full((N_CAND_PAD - N_CAND, tile), -1.0, F32)
        candn_ref[N_CAND:, :] = jnp.full((N_CAND_PAD - N_CAND, tile), -1.0, F32)
        for hd in range(PEER_HEADS):
            es = []
            for p in range(2):
                base = (hd * 2 + p) * PEER_HALF
                sc = _dot_hi(sk_ref[hd * 2 + p], qt[base:base + PEER_HALF, :])
                e = jnp.exp(sc - jnp.max(sc, axis=0, keepdims=True))
                es.append(e)
                for r, m in enumerate(_top_rows(e, PEER_TOPK)):
                    tv_ref[p, r:r + 1, :] = m
            for i, (a, b) in enumerate(PEER_PAIRS):
                cand_ref[i:i + 1, :] = tv_ref[0, a:a + 1, :] * tv_ref[1, b:b + 1, :]
            cand = cand_ref[...]
            tau = _top_rows(cand, PEER_TOPK)[-1]
            sel = cand >= tau
            inv_z = 0.5 / jnp.sum(jnp.where(sel, cand, 0.0), axis=0, keepdims=True)
            e1s_ref[hd] = es[0] * inv_z
            for lg in range(tile // LANES):
                e2_ref[hd, lg] = es[1][:, lg * LANES:(lg + 1) * LANES]
            t1s = tv_ref[0] * inv_z
            for i, (a, b) in enumerate(PEER_PAIRS):
                candn_ref[i:i + 1, :] = t1s[a:a + 1, :] * tv_ref[1, b:b + 1, :]
            g16_ref[hd] = jnp.min(jnp.where(sel, candn_ref[...], jnp.inf), axis=0, keepdims=True)
        acc_ref[...] = jnp.zeros_like(acc_ref)

    npc = eb // PEER_NKEYS
    nlg = tile // LANES
    rc = PEER_NKEYS // 2
    erf_scale = 1.0 / math.sqrt(2.0)

    pre = _dot(u_ref[...], xnt_ref[...])
    for lg in range(nlg):
        pre_ref[lg] = pre[:, lg * LANES:(lg + 1) * LANES]
    for a in range(npc):
        e1full = [e1s_ref[hd, pl.ds(s * npc + a, 1), :] for hd in range(PEER_HEADS)]
        for lg in range(nlg):
            cols = slice(lg * LANES, (lg + 1) * LANES)
            e1rows = [row[:, cols] for row in e1full]
            g16rows = [g16_ref[hd, :, cols] for hd in range(PEER_HEADS)]
            for r0 in range(0, PEER_NKEYS, rc):
                gate = None
                for hd in range(PEER_HEADS):
                    val = e2_ref[hd, lg, r0:r0 + rc, :] * e1rows[hd]
                    contrib = jnp.where(val >= g16rows[hd], val, 0.0)
                    gate = contrib if gate is None else gate + contrib
                rows = slice(a * PEER_NKEYS + r0, a * PEER_NKEYS + r0 + rc)
                pa = pre_ref[lg, rows, :]
                act_ref[lg, rows, :] = (gate * (pa * (1.0 + lax.erf(pa * erf_scale)))).astype(BF16)
    act = jnp.concatenate([act_ref[lg] for lg in range(nlg)], axis=1)
    acc_ref[...] += _dot(vt_ref[...], act)

    @pl.when(s == pl.num_programs(1) - 1)
    def _():
        o_ref[...] = h_ref[...] + acc_ref[...].T


def _peer(h, g, wqt_hi, wqt_lo, sk, u, v):
    n = h.shape[0]
    n_exp = u.shape[0]
    tile, eb = T_PEER, EB_PEER
    ns = n_exp // eb
    dq = wqt_hi.shape[0]
    vt = v.reshape(ns, eb, D_MODEL).transpose(0, 2, 1).astype(BF16)
    return pl.pallas_call(
        functools.partial(_peer_kernel, tile=tile, eb=eb),
        grid=(n // tile, ns),
        in_specs=[
            pl.BlockSpec((tile, D_MODEL), lambda i, s: (i, 0)),
            pl.BlockSpec((1, D_MODEL), lambda i, s: (0, 0)),
            pl.BlockSpec((dq, D_MODEL), lambda i, s: (0, 0)),
            pl.BlockSpec((dq, D_MODEL), lambda i, s: (0, 0)),
            pl.BlockSpec((2 * PEER_HEADS, PEER_NKEYS, PEER_HALF), lambda i, s: (0, 0, 0)),
            pl.BlockSpec((eb, D_MODEL), lambda i, s: (s, 0)),
            pl.BlockSpec((None, D_MODEL, eb), lambda i, s: (s, 0, 0)),
        ],
        out_specs=pl.BlockSpec((tile, D_MODEL), lambda i, s: (i, 0)),
        out_shape=jax.ShapeDtypeStruct((n, D_MODEL), F32),
        scratch_shapes=[
            pltpu.VMEM((D_MODEL, tile), BF16),
            pltpu.VMEM((PEER_HEADS, tile // LANES, PEER_NKEYS, LANES), F32),
            pltpu.VMEM((PEER_HEADS, PEER_NKEYS, tile), F32),
            pltpu.VMEM((PEER_HEADS, 1, tile), F32),
            pltpu.VMEM((2, PEER_TOPK, tile), F32),
            pltpu.VMEM((N_CAND_PAD, tile), F32),
            pltpu.VMEM((N_CAND_PAD, tile), F32),
            pltpu.VMEM((tile // LANES, eb, LANES), F32),
            pltpu.VMEM((tile // LANES, eb, LANES), BF16),
            pltpu.VMEM((D_MODEL, tile), F32),
        ],
        compiler_params=_cparams(("parallel", "arbitrary")),
        name="peer",
    )(h, g, wqt_hi, wqt_lo, sk, u, vt)


def _ple_kernel(h_ref, p_ref, g_ref, wg_ref, wp_ref, fg_ref, o_ref, *, final):
    h = h_ref[...]
    gate = jax.nn.sigmoid(_dot(_rms(h, g_ref[...]).astype(BF16), wg_ref[...]))
    out = h + gate * _dot(p_ref[...].astype(BF16), wp_ref[...])
    o_ref[...] = _rms(out, fg_ref[...]) if final else out


def _ple(h, p, g, wg, wp, fg, final):
    n = h.shape[0]
    tm = TM_PROJ
    const = lambda i: (0, 0)
    return pl.pallas_call(
        functools.partial(_ple_kernel, final=final),
        grid=(n // tm,),
        in_specs=[
            pl.BlockSpec((tm, D_MODEL), lambda i: (i, 0)),
            pl.BlockSpec((tm, PLE_DIM), lambda i: (i, 0)),
            pl.BlockSpec((1, D_MODEL), const),
            pl.BlockSpec((D_MODEL, D_MODEL), const),
            pl.BlockSpec((PLE_DIM, D_MODEL), const),
            pl.BlockSpec((1, D_MODEL), const),
        ],
        out_specs=pl.BlockSpec((tm, D_MODEL), lambda i: (i, 0)),
        out_shape=jax.ShapeDtypeStruct((n, D_MODEL), F32),
        compiler_params=_cparams(("parallel",)),
        name="ple",
    )(h, p, g, wg, wp, fg)


def _pad_heads(w, width=SLOT):
    k = w.shape[0]
    w = w.reshape(k, N_HEADS, -1)
    return jnp.pad(w, ((0, 0), (0, 0), (0, width - w.shape[2]))).reshape(k, N_HEADS * width)


def _rot_partner(w_rot):
    half = MLA_ROPE // 2
    return jnp.concatenate([-w_rot[..., half:], w_rot[..., :half]], axis=-1)


def _layer_weights(w_in, w_uq, w_ukv):
    offs = [int(o) for o in np.cumsum(IN_SPLITS)[:-1]]
    fq, fk, fv, fl, sq, sk, sv, cq, ckv, kr = jnp.split(w_in, offs, axis=1)
    att_scale = HEAD_DIM ** -0.5
    wb = jnp.concatenate([_pad_heads(fq * att_scale), _pad_heads(fk),
                          _pad_heads(sq * att_scale), _pad_heads(sk), fv, sv], axis=1).astype(BF16)
    z = lambda c: jnp.zeros((D_MODEL, c), F32)
    kra = jnp.concatenate([z(MLA_NOPE), kr, z(SLOT - MLA_NOPE - MLA_ROPE)], axis=1)
    krb = jnp.concatenate([z(MLA_NOPE), _rot_partner(kr), z(SLOT - MLA_NOPE - MLA_ROPE)], axis=1)
    wf = jnp.concatenate([cq, ckv, kra, krb, fl, z(LANES - N_HEADS)], axis=1).astype(BF16)
    uq = w_uq.reshape(MLA_Q_RANK, N_HEADS, MLA_NOPE + MLA_ROPE)
    zq = jnp.zeros((MLA_Q_RANK, N_HEADS, SLOT - MLA_NOPE - MLA_ROPE), F32)
    wqa = jnp.concatenate([uq, zq], axis=2).reshape(MLA_Q_RANK, N_HEADS * SLOT).astype(BF16)
    wqb = jnp.concatenate([jnp.zeros_like(uq[..., :MLA_NOPE]), _rot_partner(uq[..., MLA_NOPE:]), zq],
                          axis=2).reshape(MLA_Q_RANK, N_HEADS * SLOT).astype(BF16)
    ukv = w_ukv.reshape(MLA_KV_RANK, N_HEADS, MLA_NOPE + MLA_V)
    wk = jnp.pad(ukv[..., :MLA_NOPE], ((0, 0), (0, 0), (0, SLOT - MLA_NOPE))).reshape(MLA_KV_RANK, -1).astype(BF16)
    wv = ukv[..., MLA_NOPE:].reshape(MLA_KV_RANK, -1).astype(BF16)
    return wb, wf, wqa, wqb, wk, wv


def _rope_lane_freqs():
    half = MLA_ROPE // 2
    inv_freq = ROPE_THETA ** (-jnp.arange(half, dtype=F32) / half)
    zeros = lambda c: jnp.zeros((c,), F32)
    return jnp.concatenate([zeros(MLA_NOPE), inv_freq, inv_freq, zeros(SLOT - MLA_NOPE - MLA_ROPE)]).reshape(1, SLOT)


def kernel(x, p, positions, norm_mix_g, w_in, b_forget, mla_q_norm_g, w_uq, mla_kv_norm_g, w_ukv, mix_out_norm_g, w_o, norm_ffn_g, peer_w_query, peer_sub_keys, peer_u, peer_v, w_ple, ple_norm_g, w_ple_gate, final_norm_g):
    batch, s_len, d = x.shape
    depth = p.shape[0]
    n = batch * s_len
    h = x.reshape(n, d)
    pos = positions.reshape(n, 1)
    invf = _rope_lane_freqs()
    row = lambda v: v.reshape(1, -1)
    for i in range(depth):
        wb, wf, wqa, wqb, wk, wv = _layer_weights(w_in[i], w_uq[i], w_ukv[i])
        ob, of = _inproj(h, row(norm_mix_g[i]), wb, wf)
        bias = jnp.pad(b_forget[i], (0, LANES - N_HEADS)).reshape(1, LANES)
        fqk = _foxprep(ob, of, bias, batch, s_len)
        y_fox = _softmax_attn(fqk, 0, fqk, 1, ob, OB_FV // (GW // 2), batch, s_len, True)
        y_sb = _sb_attn(ob, batch, s_len)
        mq, mk, mv = _mla_proj(of, pos, invf, row(mla_q_norm_g[i]), row(mla_kv_norm_g[i]), wqa, wqb, wk, wv)
        y_mla = _softmax_attn(mq, 0, mk, 0, mv, 0, batch, s_len, False)
        h = _outproj(y_fox, y_sb, y_mla, row(mix_out_norm_g[i]), w_o[i].astype(BF16), h)
        sk = peer_sub_keys[i].reshape(2 * PEER_HEADS, PEER_NKEYS, PEER_HALF)
        wqt = peer_w_query[i].T
        wqt_hi = wqt.astype(BF16)
        wqt_lo = (wqt - wqt_hi.astype(F32)).astype(BF16)
        h = _peer(h, row(norm_ffn_g[i]), wqt_hi, wqt_lo, sk, peer_u[i].astype(BF16), peer_v[i])
        h = _ple(h, p[i].reshape(n, PLE_DIM), row(ple_norm_g[i]), w_ple_gate[i].astype(BF16),
                 w_ple[i].astype(BF16), row(final_norm_g), i == depth - 1)
    return h.reshape(batch, s_len, d)
```

```python
import functools
import math

import numpy as np
import jax
import jax.numpy as jnp
from jax import lax
from jax.experimental import pallas as pl
from jax.experimental.pallas import tpu as pltpu

F32 = jnp.float32
BF16 = jnp.bfloat16

D_MODEL = 1024
HEAD_DIM = 64
N_HEADS = 4
MLA_Q_RANK = 256
MLA_KV_RANK = 128
MLA_NOPE = 64
MLA_ROPE = 32
MLA_V = 128
ROPE_THETA = 10000.0
IN_SPLITS = (256, 256, 256, 4, 256, 256, 256, MLA_Q_RANK, MLA_KV_RANK, MLA_ROPE)
PEER_HEADS = 8
PEER_NKEYS = 128
PEER_HALF = 64
PEER_TOPK = 16
PLE_DIM = 256
EPS = 1e-6

LANES = 128
SLOT = LANES
VMEM_LIMIT = 56 * 1024 * 1024

TM_PROJ = 512
TQ_SOFTMAX = 512
TQ_SB = 256
TK_SB = 128
T_PEER = 512
EB_PEER = 2048

OB_FQ, OB_FK, OB_SQ, OB_SK, OB_FV, OB_SV, OB_W = 0, 512, 1024, 1536, 2048, 2304, 2560
GW = N_HEADS * SLOT
OF_CQ, OF_CKV, OF_KRA, OF_KRB, OF_FL, OF_W = 0, 256, 384, 512, 640, 768

PEER_PAIRS = tuple((a, b) for a in range(PEER_TOPK) for b in range(PEER_TOPK) if (a + 1) * (b + 1) <= PEER_TOPK)
N_CAND = len(PEER_PAIRS)
N_CAND_PAD = 56


def _cparams(sem):
    return pltpu.CompilerParams(dimension_semantics=sem, vmem_limit_bytes=VMEM_LIMIT)


def _rms(x, g):
    return x * lax.rsqrt(jnp.mean(x * x, axis=-1, keepdims=True) + EPS) * g


def _dot(a, b):
    return jnp.dot(a, b, preferred_element_type=F32)


def _dot_nt(a, b):
    return lax.dot_general(a, b, (((1,), (1,)), ((), ())), preferred_element_type=F32)


def _dot_hi(a, b):
    return jnp.dot(a, b, preferred_element_type=F32, precision=lax.Precision.HIGHEST)


def _inproj_kernel(x_ref, g_ref, wb_ref, wf_ref, ob_ref, of_ref):
    xb = _rms(x_ref[...], g_ref[...]).astype(BF16)
    ob_ref[...] = _dot(xb, wb_ref[...]).astype(BF16)
    of_ref[...] = _dot(xb, wf_ref[...])


def _inproj(h, g, wb, wf):
    n = h.shape[0]
    tm = TM_PROJ
    return pl.pallas_call(
        _inproj_kernel,
        grid=(n // tm,),
        in_specs=[
            pl.BlockSpec((tm, D_MODEL), lambda i: (i, 0)),
            pl.BlockSpec((1, D_MODEL), lambda i: (0, 0)),
            pl.BlockSpec((D_MODEL, OB_W), lambda i: (0, 0)),
            pl.BlockSpec((D_MODEL, OF_W), lambda i: (0, 0)),
        ],
        out_specs=[
            pl.BlockSpec((tm, OB_W), lambda i: (i, 0)),
            pl.BlockSpec((tm, OF_W), lambda i: (i, 0)),
        ],
        out_shape=[jax.ShapeDtypeStruct((n, OB_W), BF16), jax.ShapeDtypeStruct((n, OF_W), F32)],
        compiler_params=_cparams(("parallel",)),
        name="inproj",
    )(h, g, wb, wf)


def _split3(x):
    hi = x.astype(BF16)
    r = x - hi.astype(F32)
    mid = r.astype(BF16)
    lo = (r - mid.astype(F32)).astype(BF16)
    return hi, mid, lo


def _foxprep_kernel(qk_ref, fl_ref, b_ref, o_ref, f_ref, *, rb):
    s_len = fl_ref.shape[0]
    x = fl_ref[...] + b_ref[...]
    logf = jnp.minimum(x, 0.0) - jnp.log(1.0 + jnp.exp(-jnp.abs(x)))
    hi, mid, lo = _split3(logf)
    for r in range(s_len // rb):
        row = lax.broadcasted_iota(jnp.int32, (rb, s_len), 0) + r * rb
        col = lax.broadcasted_iota(jnp.int32, (rb, s_len), 1)
        tri = jnp.where(col <= row, 1.0, 0.0).astype(BF16)
        f_ref[r * rb:(r + 1) * rb, :] = _dot(tri, hi) + _dot(tri, mid) + _dot(tri, lo)
    f = f_ref[...]
    lane = lax.broadcasted_iota(jnp.int32, (s_len, LANES), 1)
    for h in range(N_HEADS):
        fh = f[:, h:h + 1]
        fhi, fmid, flo = (t.astype(F32) for t in _split3(fh))
        qa = jnp.where(lane == 64, fhi, jnp.where(lane == 65, fmid, jnp.where(lane == 66, flo,
             jnp.where((lane >= 67) & (lane < 70), 1.0, 0.0))))
        ka = jnp.where(lane == 67, -fhi, jnp.where(lane == 68, -fmid, jnp.where(lane == 69, -flo,
             jnp.where((lane >= 64) & (lane < 67), 1.0, 0.0))))
        qs = slice(h * SLOT, (h + 1) * SLOT)
        ks = slice(N_HEADS * SLOT + h * SLOT, N_HEADS * SLOT + (h + 1) * SLOT)
        o_ref[:, qs] = (qk_ref[:, qs].astype(F32) + qa).astype(BF16)
        o_ref[:, ks] = (qk_ref[:, ks].astype(F32) + ka).astype(BF16)


def _foxprep(ob, of, bias, batch, s_len):
    n = ob.shape[0]
    w = 2 * N_HEADS * SLOT
    return pl.pallas_call(
        functools.partial(_foxprep_kernel, rb=256),
        grid=(batch,),
        in_specs=[
            pl.BlockSpec((s_len, w), lambda b: (b, 0)),
            pl.BlockSpec((s_len, LANES), lambda b: (b, OF_FL // LANES)),
            pl.BlockSpec((1, LANES), lambda b: (0, 0)),
        ],
        out_specs=pl.BlockSpec((s_len, w), lambda b: (b, 0)),
        out_shape=jax.ShapeDtypeStruct((n, w), BF16),
        scratch_shapes=[pltpu.VMEM((s_len, LANES), F32)],
        compiler_params=_cparams(("parallel",)),
        name="foxprep",
    )(ob, of, bias)


def _softmax_attn_kernel(q_ref, k_ref, v_ref, o_ref, *, tq, pair_sum):
    qi = pl.program_id(1)
    row = lax.broadcasted_iota(jnp.int32, (tq, tq), 0)
    col = lax.broadcasted_iota(jnp.int32, (tq, tq), 1)
    causal = col <= row
    slots = [slice(h * SLOT, (h + 1) * SLOT) for h in range(N_HEADS)]
    qs = [q_ref[:, hs] for hs in slots]

    ones = jnp.ones((tq, SLOT), BF16)

    def block(j, carry, masked):
        rows = pl.ds(pl.multiple_of(j * tq, tq), tq)
        ss = [_dot_nt(qs[h], k_ref[rows, slots[h]]) for h in range(N_HEADS)]
        ms, alphas, ps = [], [], []
        for h in range(N_HEADS):
            s = jnp.where(causal, ss[h], -1e30) if masked else ss[h]
            m_new = jnp.maximum(carry[h][0], jnp.max(s, axis=1, keepdims=True))
            ms.append(m_new)
            alphas.append(jnp.exp(carry[h][0] - m_new))
            ps.append(jnp.exp(s - m_new).astype(BF16))
        if pair_sum:
            pvs = []
            for pr in range(N_HEADS // 2):
                vx = jnp.concatenate([v_ref[rows, slots[pr]], ones], axis=1)
                both = _dot(jnp.concatenate([ps[2 * pr], ps[2 * pr + 1]], axis=0), vx)
                pvs += [both[:tq], both[tq:]]
        else:
            pvs = [_dot(ps[h], jnp.concatenate([v_ref[rows, slots[h]], ones], axis=1)) for h in range(N_HEADS)]
        return tuple((ms[h], alphas[h] * carry[h][1] + pvs[h][:, SLOT:], alphas[h] * carry[h][2] + pvs[h][:, :SLOT])
                     for h in range(N_HEADS))

    init = tuple((jnp.full((tq, 1), -1e30, F32), jnp.zeros((tq, SLOT), F32), jnp.zeros((tq, SLOT), F32))
                 for _ in range(N_HEADS))
    carry = lax.fori_loop(0, qi, lambda j, c: block(j, c, False), init)
    outs = [acc / l for _, l, acc in block(qi, carry, True)]
    if pair_sum:
        lane = lax.broadcasted_iota(jnp.int32, (tq, SLOT), 1)
        for pr in range(N_HEADS // 2):
            o_ref[:, slots[pr]] = jnp.where(lane < HEAD_DIM, outs[2 * pr], outs[2 * pr + 1])
    else:
        for h in range(N_HEADS):
            o_ref[:, slots[h]] = outs[h]


def _softmax_attn(q_arr, q_col, k_arr, k_col, v_arr, v_col, batch, s_len, pair_sum):
    n = q_arr.shape[0]
    tq = TQ_SOFTMAX
    nq = s_len // tq
    vw = GW // 2 if pair_sum else GW
    return pl.pallas_call(
        functools.partial(_softmax_attn_kernel, tq=tq, pair_sum=pair_sum),
        grid=(batch, nq),
        in_specs=[
            pl.BlockSpec((tq, GW), lambda b, i: (b * nq + i, q_col)),
            pl.BlockSpec((s_len, GW), lambda b, i: (b, k_col)),
            pl.BlockSpec((s_len, vw), lambda b, i: (b, v_col)),
        ],
        out_specs=pl.BlockSpec((tq, vw), lambda b, i: (b * nq + i, 0)),
        out_shape=jax.ShapeDtypeStruct((n, vw), F32),
        compiler_params=_cparams(("parallel", "arbitrary")),
        name="fox_attn" if pair_sum else "mla_attn",
    )(q_arr, k_arr, v_arr)


def _sb_attn_kernel(q_ref, k_ref, v_ref, o_ref, *, tq, tk):
    qi = pl.program_id(1)
    nz = tq // tk
    row = lax.broadcasted_iota(jnp.int32, (tq, tk), 0)
    col = lax.broadcasted_iota(jnp.int32, (tq, tk), 1)
    ur = lax.broadcasted_iota(jnp.int32, (tk, 2 * tk), 0)
    uc = lax.broadcasted_iota(jnp.int32, (tk, 2 * tk), 1)
    uo = jnp.where((uc >= tk) | (ur > uc), 1.0, 0.0).astype(BF16)
    uo2 = jnp.concatenate([uo, uo], axis=0)
    slots = [slice(h * SLOT, (h + 1) * SLOT) for h in range(N_HEADS)]
    qs = [q_ref[:, hs] for hs in slots]

    lane = lax.broadcasted_iota(jnp.int32, (tq, SLOT), 1)

    def blocks(js, carry, zones):
        cs, accs = list(carry[0]), list(carry[1])
        rows = [pl.ds(pl.multiple_of(j * tk, tk), tk) for j in js]
        stricts = [None if zn is None else (col + zn * tk) < row for zn in zones]
        lbs, hls = [], []
        for b in range(len(js)):
            for h in range(N_HEADS):
                z = _dot_nt(qs[h], k_ref[rows[b], slots[h]])
                lb = jnp.minimum(z, 0.0) - jnp.log(1.0 + jnp.exp(-jnp.abs(z)))
                lom = lb - z
                if stricts[b] is not None:
                    lom = jnp.where(stricts[b], lom, 0.0)
                hi = lom.astype(BF16)
                lo = (lom - hi.astype(F32)).astype(BF16)
                lbs.append(lb)
                hls.append(jnp.concatenate([hi, lo], axis=1))
        r = _dot(jnp.concatenate(hls, axis=0), uo2)
        for b in range(len(js)):
            ws = []
            for h in range(N_HEADS):
                rh = r[(b * N_HEADS + h) * tq:(b * N_HEADS + h + 1) * tq]
                w = jnp.exp(lbs[b * N_HEADS + h] + rh[:, :tk] + cs[h])
                if stricts[b] is not None:
                    w = jnp.where(stricts[b], w, 0.0)
                ws.append(w.astype(BF16))
                cs[h] = cs[h] + rh[:, tk:]
            for pr in range(N_HEADS // 2):
                both = _dot(jnp.concatenate([ws[2 * pr], ws[2 * pr + 1]], axis=0), v_ref[rows[b], slots[pr]])
                accs[pr] = accs[pr] + jnp.where(lane < HEAD_DIM, both[:tq], both[tq:])
        return tuple(cs), tuple(accs)

    carry = (tuple(jnp.zeros((tq, tk), F32) for _ in range(N_HEADS)),
             tuple(jnp.zeros((tq, SLOT), F32) for _ in range(N_HEADS // 2)))
    zone = list(reversed(range(nz)))
    carry = blocks([qi * nz + zi for zi in zone], carry, zone)
    n_full = qi * nz
    carry = lax.fori_loop(
        0, qi, lambda jj, cr: blocks([n_full - 1 - jj * nz - i for i in range(nz)], cr, [None] * nz), carry)
    for pr in range(N_HEADS // 2):
        o_ref[:, slots[pr]] = carry[1][pr]


def _sb_attn(ob, batch, s_len):
    n = ob.shape[0]
    tq, tk = TQ_SB, TK_SB
    nq = s_len // tq
    vw = GW // 2
    return pl.pallas_call(
        functools.partial(_sb_attn_kernel, tq=tq, tk=tk),
        grid=(batch, nq),
        in_specs=[
            pl.BlockSpec((tq, GW), lambda b, i: (b * nq + i, OB_SQ // GW)),
            pl.BlockSpec((s_len, GW), lambda b, i: (b, OB_SK // GW)),
            pl.BlockSpec((s_len, vw), lambda b, i: (b, OB_SV // vw)),
        ],
        out_specs=pl.BlockSpec((tq, vw), lambda b, i: (b * nq + i, 0)),
        out_shape=jax.ShapeDtypeStruct((n, vw), F32),
        compiler_params=_cparams(("parallel", "arbitrary")),
        name="sb_attn",
    )(ob, ob, ob)


def _mla_proj_kernel(cq_ref, ckv_ref, kra_ref, krb_ref, pos_ref, invf_ref, gq_ref, gkv_ref,
                     wqa_ref, wqb_ref, wk_ref, wv_ref, q_ref, k_ref, v_ref, *, scale):
    ang = pos_ref[...].astype(F32) * invf_ref[...]
    cs, sn = jnp.cos(ang), jnp.sin(ang)
    cqn = _rms(cq_ref[...], gq_ref[...]).astype(BF16)
    ckvn = _rms(ckv_ref[...], gkv_ref[...]).astype(BF16)
    qa, qb = _dot(cqn, wqa_ref[...]), _dot(cqn, wqb_ref[...])
    kn = _dot(ckvn, wk_ref[...])
    v_ref[...] = _dot(ckvn, wv_ref[...]).astype(BF16)
    kr = kra_ref[...] * cs + krb_ref[...] * sn
    for h in range(N_HEADS):
        hs = slice(h * SLOT, (h + 1) * SLOT)
        q_ref[:, hs] = ((qa[:, hs] * cs + qb[:, hs] * sn) * scale).astype(BF16)
        k_ref[:, hs] = (kn[:, hs] + kr).astype(BF16)


def _mla_proj(of, pos, invf, gq, gkv, wqa, wqb, wk, wv):
    n = of.shape[0]
    tm = TM_PROJ
    w = N_HEADS * SLOT
    const = lambda i: (0, 0)
    return pl.pallas_call(
        functools.partial(_mla_proj_kernel, scale=(MLA_NOPE + MLA_ROPE) ** -0.5),
        grid=(n // tm,),
        in_specs=[
            pl.BlockSpec((tm, MLA_Q_RANK), lambda i: (i, OF_CQ // MLA_Q_RANK)),
            pl.BlockSpec((tm, LANES), lambda i: (i, OF_CKV // LANES)),
            pl.BlockSpec((tm, LANES), lambda i: (i, OF_KRA // LANES)),
            pl.BlockSpec((tm, LANES), lambda i: (i, OF_KRB // LANES)),
            pl.BlockSpec((tm, 1), lambda i: (i, 0)),
            pl.BlockSpec((1, LANES), const),
            pl.BlockSpec((1, MLA_Q_RANK), const),
            pl.BlockSpec((1, MLA_KV_RANK), const),
            pl.BlockSpec((MLA_Q_RANK, w), const),
            pl.BlockSpec((MLA_Q_RANK, w), const),
            pl.BlockSpec((MLA_KV_RANK, w), const),
            pl.BlockSpec((MLA_KV_RANK, w), const),
        ],
        out_specs=[pl.BlockSpec((tm, w), lambda i: (i, 0))] * 3,
        out_shape=[jax.ShapeDtypeStruct((n, w), BF16)] * 3,
        compiler_params=_cparams(("parallel",)),
        name="mla_proj",
    )(of, of, of, of, pos, invf, gq, gkv, wqa, wqb, wk, wv)


def _outproj_kernel(yf_ref, ys_ref, ym_ref, g_ref, wo_ref, h_ref, o_ref):
    gw = yf_ref.shape[1]
    nf = _rms(yf_ref[...], g_ref[:, 0:gw]).astype(BF16)
    ns = _rms(ys_ref[...], g_ref[:, gw:2 * gw]).astype(BF16)
    nm = _rms(ym_ref[...], g_ref[:, 2 * gw:]).astype(BF16)
    acc = _dot(nf, wo_ref[0:gw, :]) + _dot(ns, wo_ref[gw:2 * gw, :]) + _dot(nm, wo_ref[2 * gw:, :])
    o_ref[...] = h_ref[...] + acc


def _outproj(yf, ys, ym, g, wo, h):
    n = h.shape[0]
    tm = TM_PROJ
    const = lambda i: (0, 0)
    return pl.pallas_call(
        _outproj_kernel,
        grid=(n // tm,),
        in_specs=[
            pl.BlockSpec((tm, yf.shape[1]), lambda i: (i, 0)),
            pl.BlockSpec((tm, ys.shape[1]), lambda i: (i, 0)),
            pl.BlockSpec((tm, ym.shape[1]), lambda i: (i, 0)),
            pl.BlockSpec((1, D_MODEL), const),
            pl.BlockSpec((D_MODEL, D_MODEL), const),
            pl.BlockSpec((tm, D_MODEL), lambda i: (i, 0)),
        ],
        out_specs=pl.BlockSpec((tm, D_MODEL), lambda i: (i, 0)),
        out_shape=jax.ShapeDtypeStruct((n, D_MODEL), F32),
        compiler_params=_cparams(("parallel",)),
        name="outproj",
    )(yf, ys, ym, g, wo, h)


def _top_rows(e, n):
    rows = []
    for r in range(n):
        m = jnp.max(e, axis=0, keepdims=True)
        rows.append(m)
        if r + 1 < n:
            e = jnp.where(e == m, -1.0, e)
    return rows


def _peer_kernel(*refs, tile, eb):
    (h_ref, g_ref, wqh_ref, wql_ref, sk_ref, u_ref, vt_ref,
     o_ref, xnt_ref, e2_ref, e1s_ref, g16_ref, tv_ref, cand_ref, candn_ref,
     pre_ref, act_ref, acc_ref) = refs
    s = pl.program_id(1)

    @pl.when(s == 0)
    def _():
        xn = _rms(h_ref[...], g_ref[...])
        xnt = xn.T
        x_hi = xnt.astype(BF16)
        x_lo = (xnt - x_hi.astype(F32)).astype(BF16)
        xnt_ref[...] = x_hi
        qt = _dot(wqh_ref[...], x_hi) + (_dot(wqh_ref[...], x_lo) + _dot(wql_ref[...], x_hi))
        cand_ref[N_CAND:, :] = jnp.full((N_CAND_PAD - N_CAND, tile), -1.0, F32)
        candn_ref[N_CAND:, :] = jnp.full((N_CAND_PAD - N_CAND, tile), -1.0, F32)
        for hd in range(PEER_HEADS):
            es = []
            for p in range(2):
                base = (hd * 2 + p) * PEER_HALF
                sc = _dot_hi(sk_ref[hd * 2 + p], qt[base:base + PEER_HALF, :])
                e = jnp.exp(sc - jnp.max(sc, axis=0, keepdims=True))
                es.append(e)
                for r, m in enumerate(_top_rows(e, PEER_TOPK)):
                    tv_ref[p, r:r + 1, :] = m
            for i, (a, b) in enumerate(PEER_PAIRS):
                cand_ref[i:i + 1, :] = tv_ref[0, a:a + 1, :] * tv_ref[1, b:b + 1, :]
            cand = cand_ref[...]
            tau = _top_rows(cand, PEER_TOPK)[-1]
            sel = cand >= tau
            inv_z = 0.5 / jnp.sum(jnp.where(sel, cand, 0.0), axis=0, keepdims=True)
            e1s_ref[hd] = es[0] * inv_z
            for lg in range(tile // LANES):
                e2_ref[hd, lg] = es[1][:, lg * LANES:(lg + 1) * LANES]
            t1s = tv_ref[0] * inv_z
            for i, (a, b) in enumerate(PEER_PAIRS):
                candn_ref[i:i + 1, :] = t1s[a:a + 1, :] * tv_ref[1, b:b + 1, :]
            g16_ref[hd] = jnp.min(jnp.where(sel, candn_ref[...], jnp.inf), axis=0, keepdims=True)
        acc_ref[...] = jnp.zeros_like(acc_ref)

    npc = eb // PEER_NKEYS
    nlg = tile // LANES
    rc = PEER_NKEYS // 2
    erf_scale = 1.0 / math.sqrt(2.0)

    pre = _dot(u_ref[...], xnt_ref[...])
    for lg in range(nlg):
        pre_ref[lg] = pre[:, lg * LANES:(lg + 1) * LANES]
    for a in range(npc):
        e1full = [e1s_ref[hd, pl.ds(s * npc + a, 1), :] for hd in range(PEER_HEADS)]
        for lg in range(nlg):
            cols = slice(lg * LANES, (lg + 1) * LANES)
            e1rows = [row[:, cols] for row in e1full]
            g16rows = [g16_ref[hd, :, cols] for hd in range(PEER_HEADS)]
            for r0 in range(0, PEER_NKEYS, rc):
                gate = None
                for hd in range(PEER_HEADS):
                    val = e2_ref[hd, lg, r0:r0 + rc, :] * e1rows[hd]
                    contrib = jnp.where(val >= g16rows[hd], val, 0.0)
                    gate = contrib if gate is None else gate + contrib
                rows = slice(a * PEER_NKEYS + r0, a * PEER_NKEYS + r0 + rc)
                pa = pre_ref[lg, rows, :]
                act_ref[lg, rows, :] = (gate * (pa * (1.0 + lax.erf(pa * erf_scale)))).astype(BF16)
    act = jnp.concatenate([act_ref[lg] for lg in range(nlg)], axis=1)
    acc_ref[...] += _dot(vt_ref[...], act)

    @pl.when(s == pl.num_programs(1) - 1)
    def _():
        o_ref[...] = h_ref[...] + acc_ref[...].T


def _peer(h, g, wqt_hi, wqt_lo, sk, u, v):
    n = h.shape[0]
    n_exp = u.shape[0]
    tile, eb = T_PEER, EB_PEER
    ns = n_exp // eb
    dq = wqt_hi.shape[0]
    vt = v.reshape(ns, eb, D_MODEL).transpose(0, 2, 1).astype(BF16)
    return pl.pallas_call(
        functools.partial(_peer_kernel, tile=tile, eb=eb),
        grid=(n // tile, ns),
        in_specs=[
            pl.BlockSpec((tile, D_MODEL), lambda i, s: (i, 0)),
            pl.BlockSpec((1, D_MODEL), lambda i, s: (0, 0)),
            pl.BlockSpec((dq, D_MODEL), lambda i, s: (0, 0)),
            pl.BlockSpec((dq, D_MODEL), lambda i, s: (0, 0)),
            pl.BlockSpec((2 * PEER_HEADS, PEER_NKEYS, PEER_HALF), lambda i, s: (0, 0, 0)),
            pl.BlockSpec((eb, D_MODEL), lambda i, s: (s, 0)),
            pl.BlockSpec((None, D_MODEL, eb), lambda i, s: (s, 0, 0)),
        ],
        out_specs=pl.BlockSpec((tile, D_MODEL), lambda i, s: (i, 0)),
        out_shape=jax.ShapeDtypeStruct((n, D_MODEL), F32),
        scratch_shapes=[
            pltpu.VMEM((D_MODEL, tile), BF16),
            pltpu.VMEM((PEER_HEADS, tile // LANES, PEER_NKEYS, LANES), F32),
            pltpu.VMEM((PEER_HEADS, PEER_NKEYS, tile), F32),
            pltpu.VMEM((PEER_HEADS, 1, tile), F32),
            pltpu.VMEM((2, PEER_TOPK, tile), F32),
            pltpu.VMEM((N_CAND_PAD, tile), F32),
            pltpu.VMEM((N_CAND_PAD, tile), F32),
            pltpu.VMEM((tile // LANES, eb, LANES), F32),
            pltpu.VMEM((tile // LANES, eb, LANES), BF16),
            pltpu.VMEM((D_MODEL, tile), F32),
        ],
        compiler_params=_cparams(("parallel", "arbitrary")),
        name="peer",
    )(h, g, wqt_hi, wqt_lo, sk, u, vt)


def _ple_kernel(h_ref, p_ref, g_ref, wg_ref, wp_ref, fg_ref, o_ref, *, final):
    h = h_ref[...]
    gate = jax.nn.sigmoid(_dot(_rms(h, g_ref[...]).astype(BF16), wg_ref[...]))
    out = h + gate * _dot(p_ref[...].astype(BF16), wp_ref[...])
    o_ref[...] = _rms(out, fg_ref[...]) if final else out


def _ple(h, p, g, wg, wp, fg, final):
    n = h.shape[0]
    tm = TM_PROJ
    const = lambda i: (0, 0)
    return pl.pallas_call(
        functools.partial(_ple_kernel, final=final),
        grid=(n // tm,),
        in_specs=[
            pl.BlockSpec((tm, D_MODEL), lambda i: (i, 0)),
            pl.BlockSpec((tm, PLE_DIM), lambda i: (i, 0)),
            pl.BlockSpec((1, D_MODEL), const),
            pl.BlockSpec((D_MODEL, D_MODEL), const),
            pl.BlockSpec((PLE_DIM, D_MODEL), const),
            pl.BlockSpec((1, D_MODEL), const),
        ],
        out_specs=pl.BlockSpec((tm, D_MODEL), lambda i: (i, 0)),
        out_shape=jax.ShapeDtypeStruct((n, D_MODEL), F32),
        compiler_params=_cparams(("parallel",)),
        name="ple",
    )(h, p, g, wg, wp, fg)


def _pad_heads(w, width=SLOT):
    k = w.shape[0]
    w = w.reshape(k, N_HEADS, -1)
    return jnp.pad(w, ((0, 0), (0, 0), (0, width - w.shape[2]))).reshape(k, N_HEADS * width)


def _rot_partner(w_rot):
    half = MLA_ROPE // 2
    return jnp.concatenate([-w_rot[..., half:], w_rot[..., :half]], axis=-1)


def _layer_weights(w_in, w_uq, w_ukv):
    offs = [int(o) for o in np.cumsum(IN_SPLITS)[:-1]]
    fq, fk, fv, fl, sq, sk, sv, cq, ckv, kr = jnp.split(w_in, offs, axis=1)
    att_scale = HEAD_DIM ** -0.5
    wb = jnp.concatenate([_pad_heads(fq * att_scale), _pad_heads(fk),
                          _pad_heads(sq * att_scale), _pad_heads(sk), fv, sv], axis=1).astype(BF16)
    z = lambda c: jnp.zeros((D_MODEL, c), F32)
    kra = jnp.concatenate([z(MLA_NOPE), kr, z(SLOT - MLA_NOPE - MLA_ROPE)], axis=1)
    krb = jnp.concatenate([z(MLA_NOPE), _rot_partner(kr), z(SLOT - MLA_NOPE - MLA_ROPE)], axis=1)
    wf = jnp.concatenate([cq, ckv, kra, krb, fl, z(LANES - N_HEADS)], axis=1).astype(BF16)
    uq = w_uq.reshape(MLA_Q_RANK, N_HEADS, MLA_NOPE + MLA_ROPE)
    zq = jnp.zeros((MLA_Q_RANK, N_HEADS, SLOT - MLA_NOPE - MLA_ROPE), F32)
    wqa = jnp.concatenate([uq, zq], axis=2).reshape(MLA_Q_RANK, N_HEADS * SLOT).astype(BF16)
    wqb = jnp.concatenate([jnp.zeros_like(uq[..., :MLA_NOPE]), _rot_partner(uq[..., MLA_NOPE:]), zq],
                          axis=2).reshape(MLA_Q_RANK, N_HEADS * SLOT).astype(BF16)
    ukv = w_ukv.reshape(MLA_KV_RANK, N_HEADS, MLA_NOPE + MLA_V)
    wk = jnp.pad(ukv[..., :MLA_NOPE], ((0, 0), (0, 0), (0, SLOT - MLA_NOPE))).reshape(MLA_KV_RANK, -1).astype(BF16)
    wv = ukv[..., MLA_NOPE:].reshape(MLA_KV_RANK, -1).astype(BF16)
    return wb, wf, wqa, wqb, wk, wv


def _rope_lane_freqs():
    half = MLA_ROPE // 2
    inv_freq = ROPE_THETA ** (-jnp.arange(half, dtype=F32) / half)
    zeros = lambda c: jnp.zeros((c,), F32)
    return jnp.concatenate([zeros(MLA_NOPE), inv_freq, inv_freq, zeros(SLOT - MLA_NOPE - MLA_ROPE)]).reshape(1, SLOT)


def kernel(x, p, positions, norm_mix_g, w_in, b_forget, mla_q_norm_g, w_uq, mla_kv_norm_g, w_ukv, mix_out_norm_g, w_o, norm_ffn_g, peer_w_query, peer_sub_keys, peer_u, peer_v, w_ple, ple_norm_g, w_ple_gate, final_norm_g):
    batch, s_len, d = x.shape
    depth = p.shape[0]
    n = batch * s_len
    h = x.reshape(n, d)
    pos = positions.reshape(n, 1)
    invf = _rope_lane_freqs()
    row = lambda v: v.reshape(1, -1)
    for i in range(depth):
        wb, wf, wqa, wqb, wk, wv = _layer_weights(w_in[i], w_uq[i], w_ukv[i])
        ob, of = _inproj(h, row(norm_mix_g[i]), wb, wf)
        bias = jnp.pad(b_forget[i], (0, LANES - N_HEADS)).reshape(1, LANES)
        fqk = _foxprep(ob, of, bias, batch, s_len)
        y_fox = _softmax_attn(fqk, 0, fqk, 1, ob, OB_FV // (GW // 2), batch, s_len, True)
        y_sb = _sb_attn(ob, batch, s_len)
        mq, mk, mv = _mla_proj(of, pos, invf, row(mla_q_norm_g[i]), row(mla_kv_norm_g[i]), wqa, wqb, wk, wv)
        y_mla = _softmax_attn(mq, 0, mk, 0, mv, 0, batch, s_len, False)
        h = _outproj(y_fox, y_sb, y_mla, row(mix_out_norm_g[i]), w_o[i].astype(BF16), h)
        sk = peer_sub_keys[i].reshape(2 * PEER_HEADS, PEER_NKEYS, PEER_HALF)
        wqt = peer_w_query[i].T
        wqt_hi = wqt.astype(BF16)
        wqt_lo = (wqt - wqt_hi.astype(F32)).astype(BF16)
        h = _peer(h, row(norm_ffn_g[i]), wqt_hi, wqt_lo, sk, peer_u[i].astype(BF16), peer_v[i])
        h = _ple(h, p[i].reshape(n, PLE_DIM), row(ple_norm_g[i]), w_ple_gate[i].astype(BF16),
                 w_ple[i].astype(BF16), row(final_norm_g), i == depth - 1)
    return h.reshape(batch, s_len, d)
```

```python
import functools
import math

import numpy as np
import jax
import jax.numpy as jnp
from jax import lax
from jax.experimental import pallas as pl
from jax.experimental.pallas import tpu as pltpu

F32 = jnp.float32
BF16 = jnp.bfloat16

D_MODEL = 1024
HEAD_DIM = 64
N_HEADS = 4
MLA_Q_RANK = 256
MLA_KV_RANK = 128
MLA_NOPE = 64
MLA_ROPE = 32
MLA_V = 128
ROPE_THETA = 10000.0
IN_SPLITS = (256, 256, 256, 4, 256, 256, 256, MLA_Q_RANK, MLA_KV_RANK, MLA_ROPE)
PEER_HEADS = 8
PEER_NKEYS = 128
PEER_HALF = 64
PEER_TOPK = 16
PLE_DIM = 256
EPS = 1e-6

LANES = 128
SLOT = LANES
VMEM_LIMIT = 56 * 1024 * 1024

TM_PROJ = 512
TQ_SOFTMAX = 512
TQ_SB = 256
TK_SB = 128
T_PEER = 512
EB_PEER = 1024

OB_FQ, OB_FK, OB_SQ, OB_SK, OB_FV, OB_SV, OB_W = 0, 512, 1024, 1536, 2048, 2304, 2560
GW = N_HEADS * SLOT
OF_CQ, OF_CKV, OF_KRA, OF_KRB, OF_FL, OF_W = 0, 256, 384, 512, 640, 768

PEER_PAIRS = tuple((a, b) for a in range(PEER_TOPK) for b in range(PEER_TOPK) if (a + 1) * (b + 1) <= PEER_TOPK)
N_CAND = len(PEER_PAIRS)
N_CAND_PAD = 56


def _cparams(sem):
    return pltpu.CompilerParams(dimension_semantics=sem, vmem_limit_bytes=VMEM_LIMIT)


def _rms(x, g):
    return x * lax.rsqrt(jnp.mean(x * x, axis=-1, keepdims=True) + EPS) * g


def _dot(a, b):
    return jnp.dot(a, b, preferred_element_type=F32)


def _dot_nt(a, b):
    return lax.dot_general(a, b, (((1,), (1,)), ((), ())), preferred_element_type=F32)


def _inproj_kernel(x_ref, g_ref, wb_ref, wf_ref, ob_ref, of_ref):
    xb = _rms(x_ref[...], g_ref[...]).astype(BF16)
    ob_ref[...] = _dot(xb, wb_ref[...]).astype(BF16)
    of_ref[...] = _dot(xb, wf_ref[...])


def _inproj(h, g, wb, wf):
    n = h.shape[0]
    tm = TM_PROJ
    return pl.pallas_call(
        _inproj_kernel,
        grid=(n // tm,),
        in_specs=[
            pl.BlockSpec((tm, D_MODEL), lambda i: (i, 0)),
            pl.BlockSpec((1, D_MODEL), lambda i: (0, 0)),
            pl.BlockSpec((D_MODEL, OB_W), lambda i: (0, 0)),
            pl.BlockSpec((D_MODEL, OF_W), lambda i: (0, 0)),
        ],
        out_specs=[
            pl.BlockSpec((tm, OB_W), lambda i: (i, 0)),
            pl.BlockSpec((tm, OF_W), lambda i: (i, 0)),
        ],
        out_shape=[jax.ShapeDtypeStruct((n, OB_W), BF16), jax.ShapeDtypeStruct((n, OF_W), F32)],
        compiler_params=_cparams(("parallel",)),
        name="inproj",
    )(h, g, wb, wf)


def _split3(x):
    hi = x.astype(BF16)
    r = x - hi.astype(F32)
    mid = r.astype(BF16)
    lo = (r - mid.astype(F32)).astype(BF16)
    return hi, mid, lo


def _foxprep_kernel(qk_ref, fl_ref, b_ref, o_ref, f_ref, *, rb):
    s_len = fl_ref.shape[0]
    x = fl_ref[...] + b_ref[...]
    logf = jnp.minimum(x, 0.0) - jnp.log(1.0 + jnp.exp(-jnp.abs(x)))
    hi, mid, lo = _split3(logf)
    for r in range(s_len // rb):
        row = lax.broadcasted_iota(jnp.int32, (rb, s_len), 0) + r * rb
        col = lax.broadcasted_iota(jnp.int32, (rb, s_len), 1)
        tri = jnp.where(col <= row, 1.0, 0.0).astype(BF16)
        f_ref[r * rb:(r + 1) * rb, :] = _dot(tri, hi) + _dot(tri, mid) + _dot(tri, lo)
    f = f_ref[...]
    lane = lax.broadcasted_iota(jnp.int32, (s_len, LANES), 1)
    for h in range(N_HEADS):
        fh = f[:, h:h + 1]
        fhi, fmid, flo = (t.astype(F32) for t in _split3(fh))
        qa = jnp.where(lane == 64, fhi, jnp.where(lane == 65, fmid, jnp.where(lane == 66, flo,
             jnp.where((lane >= 67) & (lane < 70), 1.0, 0.0))))
        ka = jnp.where(lane == 67, -fhi, jnp.where(lane == 68, -fmid, jnp.where(lane == 69, -flo,
             jnp.where((lane >= 64) & (lane < 67), 1.0, 0.0))))
        qs = slice(h * SLOT, (h + 1) * SLOT)
        ks = slice(N_HEADS * SLOT + h * SLOT, N_HEADS * SLOT + (h + 1) * SLOT)
        o_ref[:, qs] = (qk_ref[:, qs].astype(F32) + qa).astype(BF16)
        o_ref[:, ks] = (qk_ref[:, ks].astype(F32) + ka).astype(BF16)


def _foxprep(ob, of, bias, batch, s_len):
    n = ob.shape[0]
    w = 2 * N_HEADS * SLOT
    return pl.pallas_call(
        functools.partial(_foxprep_kernel, rb=256),
        grid=(batch,),
        in_specs=[
            pl.BlockSpec((s_len, w), lambda b: (b, 0)),
            pl.BlockSpec((s_len, LANES), lambda b: (b, OF_FL // LANES)),
            pl.BlockSpec((1, LANES), lambda b: (0, 0)),
        ],
        out_specs=pl.BlockSpec((s_len, w), lambda b: (b, 0)),
        out_shape=jax.ShapeDtypeStruct((n, w), BF16),
        scratch_shapes=[pltpu.VMEM((s_len, LANES), F32)],
        compiler_params=_cparams(("parallel",)),
        name="foxprep",
    )(ob, of, bias)


def _softmax_attn_kernel(q_ref, k_ref, v_ref, o_ref, *, tq, pair_sum):
    qi = pl.program_id(1)
    row = lax.broadcasted_iota(jnp.int32, (tq, tq), 0)
    col = lax.broadcasted_iota(jnp.int32, (tq, tq), 1)
    causal = col <= row
    slots = [slice(h * SLOT, (h + 1) * SLOT) for h in range(N_HEADS)]
    qs = [q_ref[:, hs] for hs in slots]

    ones = jnp.ones((tq, SLOT), BF16)

    def block(j, carry, masked):
        rows = pl.ds(pl.multiple_of(j * tq, tq), tq)
        ss = [_dot_nt(qs[h], k_ref[rows, slots[h]]) for h in range(N_HEADS)]
        ms, alphas, ps = [], [], []
        for h in range(N_HEADS):
            s = jnp.where(causal, ss[h], -1e30) if masked else ss[h]
            m_new = jnp.maximum(carry[h][0], jnp.max(s, axis=1, keepdims=True))
            ms.append(m_new)
            alphas.append(jnp.exp(carry[h][0] - m_new))
            ps.append(jnp.exp(s - m_new).astype(BF16))
        if pair_sum:
            pvs = []
            for pr in range(N_HEADS // 2):
                vx = jnp.concatenate([v_ref[rows, slots[pr]], ones], axis=1)
                both = _dot(jnp.concatenate([ps[2 * pr], ps[2 * pr + 1]], axis=0), vx)
                pvs += [both[:tq], both[tq:]]
        else:
            pvs = [_dot(ps[h], jnp.concatenate([v_ref[rows, slots[h]], ones], axis=1)) for h in range(N_HEADS)]
        return tuple((ms[h], alphas[h] * carry[h][1] + pvs[h][:, SLOT:], alphas[h] * carry[h][2] + pvs[h][:, :SLOT])
                     for h in range(N_HEADS))

    init = tuple((jnp.full((tq, 1), -1e30, F32), jnp.zeros((tq, SLOT), F32), jnp.zeros((tq, SLOT), F32))
                 for _ in range(N_HEADS))
    carry = lax.fori_loop(0, qi, lambda j, c: block(j, c, False), init)
    outs = [acc / l for _, l, acc in block(qi, carry, True)]
    if pair_sum:
        lane = lax.broadcasted_iota(jnp.int32, (tq, SLOT), 1)
        for pr in range(N_HEADS // 2):
            o_ref[:, slots[pr]] = jnp.where(lane < HEAD_DIM, outs[2 * pr], outs[2 * pr + 1])
    else:
        for h in range(N_HEADS):
            o_ref[:, slots[h]] = outs[h]


def _softmax_attn(q_arr, q_col, k_arr, k_col, v_arr, v_col, batch, s_len, pair_sum):
    n = q_arr.shape[0]
    tq = TQ_SOFTMAX
    nq = s_len // tq
    vw = GW // 2 if pair_sum else GW
    return pl.pallas_call(
        functools.partial(_softmax_attn_kernel, tq=tq, pair_sum=pair_sum),
        grid=(batch, nq),
        in_specs=[
            pl.BlockSpec((tq, GW), lambda b, i: (b * nq + i, q_col)),
            pl.BlockSpec((s_len, GW), lambda b, i: (b, k_col)),
            pl.BlockSpec((s_len, vw), lambda b, i: (b, v_col)),
        ],
        out_specs=pl.BlockSpec((tq, vw), lambda b, i: (b * nq + i, 0)),
        out_shape=jax.ShapeDtypeStruct((n, vw), F32),
        compiler_params=_cparams(("parallel", "arbitrary")),
        name="fox_attn" if pair_sum else "mla_attn",
    )(q_arr, k_arr, v_arr)


def _sb_attn_kernel(q_ref, k_ref, v_ref, o_ref, *, tq, tk):
    qi = pl.program_id(1)
    nz = tq // tk
    row = lax.broadcasted_iota(jnp.int32, (tq, tk), 0)
    col = lax.broadcasted_iota(jnp.int32, (tq, tk), 1)
    ur = lax.broadcasted_iota(jnp.int32, (tk, 2 * tk), 0)
    uc = lax.broadcasted_iota(jnp.int32, (tk, 2 * tk), 1)
    uo = jnp.where((uc >= tk) | (ur > uc), 1.0, 0.0).astype(BF16)
    uo2 = jnp.concatenate([uo, uo], axis=0)
    slots = [slice(h * SLOT, (h + 1) * SLOT) for h in range(N_HEADS)]
    qs = [q_ref[:, hs] for hs in slots]

    lane = lax.broadcasted_iota(jnp.int32, (tq, SLOT), 1)

    def blocks(js, carry, zones):
        cs, accs = list(carry[0]), list(carry[1])
        rows = [pl.ds(pl.multiple_of(j * tk, tk), tk) for j in js]
        stricts = [None if zn is None else (col + zn * tk) < row for zn in zones]
        lbs, hls = [], []
        for b in range(len(js)):
            for h in range(N_HEADS):
                z = _dot_nt(qs[h], k_ref[rows[b], slots[h]])
                lb = jnp.minimum(z, 0.0) - jnp.log(1.0 + jnp.exp(-jnp.abs(z)))
                lom = lb - z
                if stricts[b] is not None:
                    lom = jnp.where(stricts[b], lom, 0.0)
                hi = lom.astype(BF16)
                lo = (lom - hi.astype(F32)).astype(BF16)
                lbs.append(lb)
                hls.append(jnp.concatenate([hi, lo], axis=1))
        r = _dot(jnp.concatenate(hls, axis=0), uo2)
        for b in range(len(js)):
            ws = []
            for h in range(N_HEADS):
                rh = r[(b * N_HEADS + h) * tq:(b * N_HEADS + h + 1) * tq]
                w = jnp.exp(lbs[b * N_HEADS + h] + rh[:, :tk] + cs[h])
                if stricts[b] is not None:
                    w = jnp.where(stricts[b], w, 0.0)
                ws.append(w.astype(BF16))
                cs[h] = cs[h] + rh[:, tk:]
            for pr in range(N_HEADS // 2):
                both = _dot(jnp.concatenate([ws[2 * pr], ws[2 * pr + 1]], axis=0), v_ref[rows[b], slots[pr]])
                accs[pr] = accs[pr] + jnp.where(lane < HEAD_DIM, both[:tq], both[tq:])
        return tuple(cs), tuple(accs)

    carry = (tuple(jnp.zeros((tq, tk), F32) for _ in range(N_HEADS)),
             tuple(jnp.zeros((tq, SLOT), F32) for _ in range(N_HEADS // 2)))
    zone = list(reversed(range(nz)))
    carry = blocks([qi * nz + zi for zi in zone], carry, zone)
    n_full = qi * nz
    carry = lax.fori_loop(
        0, qi, lambda jj, cr: blocks([n_full - 1 - jj * nz - i for i in range(nz)], cr, [None] * nz), carry)
    for pr in range(N_HEADS // 2):
        o_ref[:, slots[pr]] = carry[1][pr]


def _sb_attn(ob, batch, s_len):
    n = ob.shape[0]
    tq, tk = TQ_SB, TK_SB
    nq = s_len // tq
    vw = GW // 2
    return pl.pallas_call(
        functools.partial(_sb_attn_kernel, tq=tq, tk=tk),
        grid=(batch, nq),
        in_specs=[
            pl.BlockSpec((tq, GW), lambda b, i: (b * nq + i, OB_SQ // GW)),
            pl.BlockSpec((s_len, GW), lambda b, i: (b, OB_SK // GW)),
            pl.BlockSpec((s_len, vw), lambda b, i: (b, OB_SV // vw)),
        ],
        out_specs=pl.BlockSpec((tq, vw), lambda b, i: (b * nq + i, 0)),
        out_shape=jax.ShapeDtypeStruct((n, vw), F32),
        compiler_params=_cparams(("parallel", "arbitrary")),
        name="sb_attn",
    )(ob, ob, ob)


def _rope_tables_kernel(pos_ref, invf_ref, cs_ref, sn_ref):
    ang = pos_ref[...].astype(F32) * invf_ref[...]
    cs_ref[...] = jnp.cos(ang)
    sn_ref[...] = jnp.sin(ang)


def _rope_tables(pos, invf):
    n = pos.shape[0]
    tm = TM_PROJ
    return pl.pallas_call(
        _rope_tables_kernel,
        grid=(n // tm,),
        in_specs=[pl.BlockSpec((tm, 1), lambda i: (i, 0)), pl.BlockSpec((1, LANES), lambda i: (0, 0))],
        out_specs=[pl.BlockSpec((tm, LANES), lambda i: (i, 0))] * 2,
        out_shape=[jax.ShapeDtypeStruct((n, LANES), F32)] * 2,
        compiler_params=_cparams(("parallel",)),
        name="rope_tables",
    )(pos, invf)


def _mla_proj_kernel(cq_ref, ckv_ref, kra_ref, krb_ref, cs_ref, sn_ref, gq_ref, gkv_ref,
                     wqa_ref, wqb_ref, wk_ref, wv_ref, q_ref, k_ref, v_ref, *, scale):
    cs, sn = cs_ref[...], sn_ref[...]
    cqn = _rms(cq_ref[...], gq_ref[...]).astype(BF16)
    ckvn = _rms(ckv_ref[...], gkv_ref[...]).astype(BF16)
    qa, qb = _dot(cqn, wqa_ref[...]), _dot(cqn, wqb_ref[...])
    kn = _dot(ckvn, wk_ref[...])
    v_ref[...] = _dot(ckvn, wv_ref[...]).astype(BF16)
    kr = kra_ref[...] * cs + krb_ref[...] * sn
    for h in range(N_HEADS):
        hs = slice(h * SLOT, (h + 1) * SLOT)
        q_ref[:, hs] = ((qa[:, hs] * cs + qb[:, hs] * sn) * scale).astype(BF16)
        k_ref[:, hs] = (kn[:, hs] + kr).astype(BF16)


def _mla_proj(of, cs, sn, gq, gkv, wqa, wqb, wk, wv):
    n = of.shape[0]
    tm = TM_PROJ
    w = N_HEADS * SLOT
    const = lambda i: (0, 0)
    return pl.pallas_call(
        functools.partial(_mla_proj_kernel, scale=(MLA_NOPE + MLA_ROPE) ** -0.5),
        grid=(n // tm,),
        in_specs=[
            pl.BlockSpec((tm, MLA_Q_RANK), lambda i: (i, OF_CQ // MLA_Q_RANK)),
            pl.BlockSpec((tm, LANES), lambda i: (i, OF_CKV // LANES)),
            pl.BlockSpec((tm, LANES), lambda i: (i, OF_KRA // LANES)),
            pl.BlockSpec((tm, LANES), lambda i: (i, OF_KRB // LANES)),
            pl.BlockSpec((tm, LANES), lambda i: (i, 0)),
            pl.BlockSpec((tm, LANES), lambda i: (i, 0)),
            pl.BlockSpec((1, MLA_Q_RANK), const),
            pl.BlockSpec((1, MLA_KV_RANK), const),
            pl.BlockSpec((MLA_Q_RANK, w), const),
            pl.BlockSpec((MLA_Q_RANK, w), const),
            pl.BlockSpec((MLA_KV_RANK, w), const),
            pl.BlockSpec((MLA_KV_RANK, w), const),
        ],
        out_specs=[pl.BlockSpec((tm, w), lambda i: (i, 0))] * 3,
        out_shape=[jax.ShapeDtypeStruct((n, w), BF16)] * 3,
        compiler_params=_cparams(("parallel",)),
        name="mla_proj",
    )(of, of, of, of, cs, sn, gq, gkv, wqa, wqb, wk, wv)


def _outproj_kernel(yf_ref, ys_ref, ym_ref, g_ref, wo_ref, h_ref, o_ref):
    gw = yf_ref.shape[1]
    nf = _rms(yf_ref[...], g_ref[:, 0:gw]).astype(BF16)
    ns = _rms(ys_ref[...], g_ref[:, gw:2 * gw]).astype(BF16)
    nm = _rms(ym_ref[...], g_ref[:, 2 * gw:]).astype(BF16)
    acc = _dot(nf, wo_ref[0:gw, :]) + _dot(ns, wo_ref[gw:2 * gw, :]) + _dot(nm, wo_ref[2 * gw:, :])
    o_ref[...] = h_ref[...] + acc


def _outproj(yf, ys, ym, g, wo, h):
    n = h.shape[0]
    tm = TM_PROJ
    const = lambda i: (0, 0)
    return pl.pallas_call(
        _outproj_kernel,
        grid=(n // tm,),
        in_specs=[
            pl.BlockSpec((tm, yf.shape[1]), lambda i: (i, 0)),
            pl.BlockSpec((tm, ys.shape[1]), lambda i: (i, 0)),
            pl.BlockSpec((tm, ym.shape[1]), lambda i: (i, 0)),
            pl.BlockSpec((1, D_MODEL), const),
            pl.BlockSpec((D_MODEL, D_MODEL), const),
            pl.BlockSpec((tm, D_MODEL), lambda i: (i, 0)),
        ],
        out_specs=pl.BlockSpec((tm, D_MODEL), lambda i: (i, 0)),
        out_shape=jax.ShapeDtypeStruct((n, D_MODEL), F32),
        compiler_params=_cparams(("parallel",)),
        name="outproj",
    )(yf, ys, ym, g, wo, h)


def _batcher_pairs(n):
    def merge(lo, hi, r):
        step = r * 2
        if step < hi - lo:
            yield from merge(lo, hi, step)
            yield from merge(lo + r, hi, step)
            yield from ((i, i + r) for i in range(lo + r, hi - r, step))
        else:
            yield (lo, lo + r)

    def sort(lo, hi):
        if hi - lo >= 1:
            mid = lo + (hi - lo) // 2
            yield from sort(lo, mid)
            yield from sort(mid + 1, hi)
            yield from merge(lo, hi, 1)

    return tuple(sort(0, n - 1))


def _compare_exchange(v, pairs):
    v = list(v)
    for i, j in pairs:
        v[i], v[j] = jnp.maximum(v[i], v[j]), jnp.minimum(v[i], v[j])
    return v


def _bitonic_pairs(n):
    return tuple((i, i + d) for d in (n >> k for k in range(1, n.bit_length())) for i in range(n) if not i & d)


def _top16_rows(slabs):
    n = PEER_TOPK
    v = _compare_exchange(slabs, _batcher_pairs(len(slabs)))
    for shift in (4, 2, 1):
        if len(v) < n:
            v = v + [pltpu.roll(x, shift, axis=0) for x in reversed(v)]
        else:
            v = [jnp.maximum(v[i], pltpu.roll(v[n - 1 - i], shift, axis=0)) for i in range(n)]
        v = _compare_exchange(v, _bitonic_pairs(n))
    return v


def _peer_kernel(*refs, tile, eb):
    (h_ref, g_ref, wq_ref, sk_ref, u_ref, vt_ref,
     o_ref, xnt_ref, e2_ref, e1s_ref, g16_ref, tv_ref, cand_ref, candn_ref,
     pre_ref, act_ref, acc_ref) = refs
    s = pl.program_id(1)

    @pl.when(s == 0)
    def _():
        xn = _rms(h_ref[...], g_ref[...])
        xnt = xn.T
        xnt_ref[...] = xnt.astype(BF16)
        qt = _dot(wq_ref[...], xnt_ref[...]).astype(BF16)
        cand_ref[N_CAND:, :] = jnp.full((N_CAND_PAD - N_CAND, tile), -1.0, F32)
        candn_ref[N_CAND:, :] = jnp.full((N_CAND_PAD - N_CAND, tile), -1.0, F32)
        for hd in range(PEER_HEADS):
            es = []
            for p in range(2):
                base = (hd * 2 + p) * PEER_HALF
                sc = _dot(sk_ref[hd * 2 + p], qt[base:base + PEER_HALF, :])
                e = jnp.exp(sc - jnp.max(sc, axis=0, keepdims=True))
                es.append(e)
                top = _top16_rows([e[8 * i:8 * i + 8, :] for i in range(PEER_NKEYS // 8)])
                for r in range(PEER_TOPK):
                    tv_ref[p, r:r + 1, :] = top[r][0:1, :]
            for i, (a, b) in enumerate(PEER_PAIRS):
                cand_ref[i:i + 1, :] = tv_ref[0, a:a + 1, :] * tv_ref[1, b:b + 1, :]
            cand = cand_ref[...]
            pad = jnp.full((8, tile), -1.0, F32)
            cslabs = [cand[8 * i:8 * i + 8, :] for i in range(N_CAND_PAD // 8)] + [pad] * (8 - N_CAND_PAD // 8)
            tau = _top16_rows(cslabs)[PEER_TOPK - 1][0:1, :]
            sel = cand >= tau
            inv_z = 0.5 / jnp.sum(jnp.where(sel, cand, 0.0), axis=0, keepdims=True)
            e1s_ref[hd] = es[0] * inv_z
            for lg in range(tile // LANES):
                e2_ref[hd, lg] = es[1][:, lg * LANES:(lg + 1) * LANES]
            t1s = tv_ref[0] * inv_z
            for i, (a, b) in enumerate(PEER_PAIRS):
                candn_ref[i:i + 1, :] = t1s[a:a + 1, :] * tv_ref[1, b:b + 1, :]
            g16_ref[hd] = jnp.min(jnp.where(sel, candn_ref[...], jnp.inf), axis=0, keepdims=True)
        acc_ref[...] = jnp.zeros_like(acc_ref)

    npc = eb // PEER_NKEYS
    nlg = tile // LANES
    rc = PEER_NKEYS // 2
    erf_scale = 1.0 / math.sqrt(2.0)

    pre = _dot(u_ref[...], xnt_ref[...])
    for lg in range(nlg):
        pre_ref[lg] = pre[:, lg * LANES:(lg + 1) * LANES]
    for a in range(npc):
        e1full = [e1s_ref[hd, pl.ds(s * npc + a, 1), :] for hd in range(PEER_HEADS)]
        for lg in range(nlg):
            cols = slice(lg * LANES, (lg + 1) * LANES)
            e1rows = [row[:, cols] for row in e1full]
            g16rows = [g16_ref[hd, :, cols] for hd in range(PEER_HEADS)]
            for r0 in range(0, PEER_NKEYS, rc):
                gate = None
                for hd in range(PEER_HEADS):
                    val = e2_ref[hd, lg, r0:r0 + rc, :] * e1rows[hd]
                    contrib = jnp.where(val >= g16rows[hd], val, 0.0)
                    gate = contrib if gate is None else gate + contrib
                rows = slice(a * PEER_NKEYS + r0, a * PEER_NKEYS + r0 + rc)
                pa = pre_ref[lg, rows, :]
                act_ref[lg, rows, :] = (gate * (pa * (1.0 + lax.erf(pa * erf_scale)))).astype(BF16)
    act = jnp.concatenate([act_ref[lg] for lg in range(nlg)], axis=1)
    acc_ref[...] += _dot(vt_ref[...], act)

    @pl.when(s == pl.num_programs(1) - 1)
    def _():
        o_ref[...] = h_ref[...] + acc_ref[...].T


def _peer(h, g, wqt, sk, u, v):
    n = h.shape[0]
    n_exp = u.shape[0]
    tile, eb = T_PEER, EB_PEER
    ns = n_exp // eb
    dq = wqt.shape[0]
    vt = v.reshape(ns, eb, D_MODEL).transpose(0, 2, 1).astype(BF16)
    return pl.pallas_call(
        functools.partial(_peer_kernel, tile=tile, eb=eb),
        grid=(n // tile, ns),
        in_specs=[
            pl.BlockSpec((tile, D_MODEL), lambda i, s: (i, 0)),
            pl.BlockSpec((1, D_MODEL), lambda i, s: (0, 0)),
            pl.BlockSpec((dq, D_MODEL), lambda i, s: (0, 0)),
            pl.BlockSpec((2 * PEER_HEADS, PEER_NKEYS, PEER_HALF), lambda i, s: (0, 0, 0)),
            pl.BlockSpec((eb, D_MODEL), lambda i, s: (s, 0)),
            pl.BlockSpec((None, D_MODEL, eb), lambda i, s: (s, 0, 0)),
        ],
        out_specs=pl.BlockSpec((tile, D_MODEL), lambda i, s: (i, 0)),
        out_shape=jax.ShapeDtypeStruct((n, D_MODEL), F32),
        scratch_shapes=[
            pltpu.VMEM((D_MODEL, tile), BF16),
            pltpu.VMEM((PEER_HEADS, tile // LANES, PEER_NKEYS, LANES), F32),
            pltpu.VMEM((PEER_HEADS, PEER_NKEYS, tile), F32),
            pltpu.VMEM((PEER_HEADS, 1, tile), F32),
            pltpu.VMEM((2, PEER_TOPK, tile), F32),
            pltpu.VMEM((N_CAND_PAD, tile), F32),
            pltpu.VMEM((N_CAND_PAD, tile), F32),
            pltpu.VMEM((tile // LANES, eb, LANES), F32),
            pltpu.VMEM((tile // LANES, eb, LANES), BF16),
            pltpu.VMEM((D_MODEL, tile), F32),
        ],
        compiler_params=_cparams(("parallel", "arbitrary")),
        name="peer",
    )(h, g, wqt, sk, u, vt)


def _ple_kernel(h_ref, p_ref, g_ref, wg_ref, wp_ref, fg_ref, o_ref, *, final):
    h = h_ref[...]
    gate = jax.nn.sigmoid(_dot(_rms(h, g_ref[...]).astype(BF16), wg_ref[...]))
    out = h + gate * _dot(p_ref[...].astype(BF16), wp_ref[...])
    o_ref[...] = _rms(out, fg_ref[...]) if final else out


def _ple(h, p, g, wg, wp, fg, final):
    n = h.shape[0]
    tm = TM_PROJ
    const = lambda i: (0, 0)
    return pl.pallas_call(
        functools.partial(_ple_kernel, final=final),
        grid=(n // tm,),
        in_specs=[
            pl.BlockSpec((tm, D_MODEL), lambda i: (i, 0)),
            pl.BlockSpec((tm, PLE_DIM), lambda i: (i, 0)),
            pl.BlockSpec((1, D_MODEL), const),
            pl.BlockSpec((D_MODEL, D_MODEL), const),
            pl.BlockSpec((PLE_DIM, D_MODEL), const),
            pl.BlockSpec((1, D_MODEL), const),
        ],
        out_specs=pl.BlockSpec((tm, D_MODEL), lambda i: (i, 0)),
        out_shape=jax.ShapeDtypeStruct((n, D_MODEL), F32),
        compiler_params=_cparams(("parallel",)),
        name="ple",
    )(h, p, g, wg, wp, fg)


def _pad_heads(w, width=SLOT):
    k = w.shape[0]
    w = w.reshape(k, N_HEADS, -1)
    return jnp.pad(w, ((0, 0), (0, 0), (0, width - w.shape[2]))).reshape(k, N_HEADS * width)


def _rot_partner(w_rot):
    half = MLA_ROPE // 2
    return jnp.concatenate([-w_rot[..., half:], w_rot[..., :half]], axis=-1)


def _layer_weights(w_in, w_uq, w_ukv):
    offs = [int(o) for o in np.cumsum(IN_SPLITS)[:-1]]
    fq, fk, fv, fl, sq, sk, sv, cq, ckv, kr = jnp.split(w_in, offs, axis=1)
    att_scale = HEAD_DIM ** -0.5
    wb = jnp.concatenate([_pad_heads(fq * att_scale), _pad_heads(fk),
                          _pad_heads(sq * att_scale), _pad_heads(sk), fv, sv], axis=1).astype(BF16)
    z = lambda c: jnp.zeros((D_MODEL, c), F32)
    kra = jnp.concatenate([z(MLA_NOPE), kr, z(SLOT - MLA_NOPE - MLA_ROPE)], axis=1)
    krb = jnp.concatenate([z(MLA_NOPE), _rot_partner(kr), z(SLOT - MLA_NOPE - MLA_ROPE)], axis=1)
    wf = jnp.concatenate([cq, ckv, kra, krb, fl, z(LANES - N_HEADS)], axis=1).astype(BF16)
    uq = w_uq.reshape(MLA_Q_RANK, N_HEADS, MLA_NOPE + MLA_ROPE)
    zq = jnp.zeros((MLA_Q_RANK, N_HEADS, SLOT - MLA_NOPE - MLA_ROPE), F32)
    wqa = jnp.concatenate([uq, zq], axis=2).reshape(MLA_Q_RANK, N_HEADS * SLOT).astype(BF16)
    wqb = jnp.concatenate([jnp.zeros_like(uq[..., :MLA_NOPE]), _rot_partner(uq[..., MLA_NOPE:]), zq],
                          axis=2).reshape(MLA_Q_RANK, N_HEADS * SLOT).astype(BF16)
    ukv = w_ukv.reshape(MLA_KV_RANK, N_HEADS, MLA_NOPE + MLA_V)
    wk = jnp.pad(ukv[..., :MLA_NOPE], ((0, 0), (0, 0), (0, SLOT - MLA_NOPE))).reshape(MLA_KV_RANK, -1).astype(BF16)
    wv = ukv[..., MLA_NOPE:].reshape(MLA_KV_RANK, -1).astype(BF16)
    return wb, wf, wqa, wqb, wk, wv


def _rope_lane_freqs():
    half = MLA_ROPE // 2
    inv_freq = ROPE_THETA ** (-jnp.arange(half, dtype=F32) / half)
    zeros = lambda c: jnp.zeros((c,), F32)
    return jnp.concatenate([zeros(MLA_NOPE), inv_freq, inv_freq, zeros(SLOT - MLA_NOPE - MLA_ROPE)]).reshape(1, SLOT)


def kernel(x, p, positions, norm_mix_g, w_in, b_forget, mla_q_norm_g, w_uq, mla_kv_norm_g, w_ukv, mix_out_norm_g, w_o, norm_ffn_g, peer_w_query, peer_sub_keys, peer_u, peer_v, w_ple, ple_norm_g, w_ple_gate, final_norm_g):
    batch, s_len, d = x.shape
    depth = p.shape[0]
    n = batch * s_len
    h = x.reshape(n, d)
    pos = positions.reshape(n, 1)
    rope_cs, rope_sn = _rope_tables(pos, _rope_lane_freqs())
    row = lambda v: v.reshape(1, -1)
    for i in range(depth):
        wb, wf, wqa, wqb, wk, wv = _layer_weights(w_in[i], w_uq[i], w_ukv[i])
        ob, of = _inproj(h, row(norm_mix_g[i]), wb, wf)
        bias = jnp.pad(b_forget[i], (0, LANES - N_HEADS)).reshape(1, LANES)
        fqk = _foxprep(ob, of, bias, batch, s_len)
        y_fox = _softmax_attn(fqk, 0, fqk, 1, ob, OB_FV // (GW // 2), batch, s_len, True)
        y_sb = _sb_attn(ob, batch, s_len)
        mq, mk, mv = _mla_proj(of, rope_cs, rope_sn, row(mla_q_norm_g[i]), row(mla_kv_norm_g[i]), wqa, wqb, wk, wv)
        y_mla = _softmax_attn(mq, 0, mk, 0, mv, 0, batch, s_len, False)
        h = _outproj(y_fox, y_sb, y_mla, row(mix_out_norm_g[i]), w_o[i].astype(BF16), h)
        sk = peer_sub_keys[i].reshape(2 * PEER_HEADS, PEER_NKEYS, PEER_HALF).astype(BF16)
        h = _peer(h, row(norm_ffn_g[i]), peer_w_query[i].T.astype(BF16), sk, peer_u[i].astype(BF16), peer_v[i])
        h = _ple(h, p[i].reshape(n, PLE_DIM), row(ple_norm_g[i]), w_ple_gate[i].astype(BF16),
                 w_ple[i].astype(BF16), row(final_norm_g), i == depth - 1)
    return h.reshape(batch, s_len, d)
```

```python
import functools
import math

import numpy as np
import jax
import jax.numpy as jnp
from jax import lax
from jax.experimental import pallas as pl
from jax.experimental.pallas import tpu as pltpu

F32 = jnp.float32
BF16 = jnp.bfloat16

D_MODEL = 1024
HEAD_DIM = 64
N_HEADS = 4
MLA_Q_RANK = 256
MLA_KV_RANK = 128
MLA_NOPE = 64
MLA_ROPE = 32
MLA_V = 128
ROPE_THETA = 10000.0
IN_SPLITS = (256, 256, 256, 4, 256, 256, 256, MLA_Q_RANK, MLA_KV_RANK, MLA_ROPE)
PEER_HEADS = 8
PEER_NKEYS = 128
PEER_HALF = 64
PEER_TOPK = 16
PLE_DIM = 256
EPS = 1e-6

LANES = 128
SLOT = LANES
VMEM_LIMIT = 56 * 1024 * 1024

TM_PROJ = 512
TQ_SOFTMAX = 512
TQ_SB = 256
TK_SB = 128
T_PEER = 512
EB_PEER = 1024

OB_FQ, OB_FK, OB_SQ, OB_SK, OB_FV, OB_SV, OB_W = 0, 512, 1024, 1536, 2048, 2304, 2560
GW = N_HEADS * SLOT
OF_CQ, OF_CKV, OF_KRA, OF_KRB, OF_FL, OF_W = 0, 256, 384, 512, 640, 768

PEER_PAIRS = tuple((a, b) for a in range(PEER_TOPK) for b in range(PEER_TOPK) if (a + 1) * (b + 1) <= PEER_TOPK)
N_CAND = len(PEER_PAIRS)
N_CAND_PAD = 56


def _cparams(sem):
    return pltpu.CompilerParams(dimension_semantics=sem, vmem_limit_bytes=VMEM_LIMIT)


def _rms(x, g):
    return x * lax.rsqrt(jnp.mean(x * x, axis=-1, keepdims=True) + EPS) * g


def _dot(a, b):
    return jnp.dot(a, b, preferred_element_type=F32)


def _dot_nt(a, b):
    return lax.dot_general(a, b, (((1,), (1,)), ((), ())), preferred_element_type=F32)


def _inproj_kernel(x_ref, g_ref, wb_ref, wf_ref, ob_ref, of_ref):
    xb = _rms(x_ref[...], g_ref[...]).astype(BF16)
    ob_ref[...] = _dot(xb, wb_ref[...]).astype(BF16)
    of_ref[...] = _dot(xb, wf_ref[...])


def _inproj(h, g, wb, wf):
    n = h.shape[0]
    tm = TM_PROJ
    return pl.pallas_call(
        _inproj_kernel,
        grid=(n // tm,),
        in_specs=[
            pl.BlockSpec((tm, D_MODEL), lambda i: (i, 0)),
            pl.BlockSpec((1, D_MODEL), lambda i: (0, 0)),
            pl.BlockSpec((D_MODEL, OB_W), lambda i: (0, 0)),
            pl.BlockSpec((D_MODEL, OF_W), lambda i: (0, 0)),
        ],
        out_specs=[
            pl.BlockSpec((tm, OB_W), lambda i: (i, 0)),
            pl.BlockSpec((tm, OF_W), lambda i: (i, 0)),
        ],
        out_shape=[jax.ShapeDtypeStruct((n, OB_W), BF16), jax.ShapeDtypeStruct((n, OF_W), F32)],
        compiler_params=_cparams(("parallel",)),
        name="inproj",
    )(h, g, wb, wf)


def _split3(x):
    hi = x.astype(BF16)
    r = x - hi.astype(F32)
    mid = r.astype(BF16)
    lo = (r - mid.astype(F32)).astype(BF16)
    return hi, mid, lo


FOX_AUG_LANE = HEAD_DIM


def _fox_placement():
    place = np.zeros((3 * LANES, 2 * GW), np.float32)
    ones = np.zeros((1, 2 * GW), np.float32)
    for h in range(N_HEADS):
        for t in range(3):
            place[t * LANES + h, h * SLOT + FOX_AUG_LANE + t] = 1.0
            place[t * LANES + h, GW + h * SLOT + FOX_AUG_LANE + 3 + t] = -1.0
            ones[0, h * SLOT + FOX_AUG_LANE + 3 + t] = 1.0
            ones[0, GW + h * SLOT + FOX_AUG_LANE + t] = 1.0
    return jnp.asarray(place, BF16), jnp.asarray(ones, F32)


def _foxprep_kernel(qk_ref, fl_ref, b_ref, place_ref, ones_ref, o_ref, f_ref, *, rb):
    s_len = fl_ref.shape[0]
    x = fl_ref[...] + b_ref[...]
    logf = jnp.minimum(x, 0.0) - jnp.log(1.0 + jnp.exp(-jnp.abs(x)))
    row = lax.broadcasted_iota(jnp.int32, (rb, rb), 0)
    col = lax.broadcasted_iota(jnp.int32, (rb, rb), 1)
    tri = jnp.where(col <= row, 1.0, 0.0).astype(BF16)
    carry = jnp.zeros((1, LANES), F32)
    for r in range(s_len // rb):
        hi, mid, lo = _split3(logf[r * rb:(r + 1) * rb, :])
        fb = _dot(tri, hi) + _dot(tri, mid) + _dot(tri, lo) + carry
        f_ref[r * rb:(r + 1) * rb, :] = fb
        carry = fb[rb - 1:rb, :]
    terms = jnp.concatenate(_split3(f_ref[...]), axis=1)
    aug = _dot(terms, place_ref[...]) + ones_ref[...]
    o_ref[...] = (qk_ref[...].astype(F32) + aug).astype(BF16)


def _foxprep(ob, of, bias, batch, s_len):
    n = ob.shape[0]
    w = 2 * GW
    place, ones = _fox_placement()
    const = lambda b: (0, 0)
    return pl.pallas_call(
        functools.partial(_foxprep_kernel, rb=256),
        grid=(batch,),
        in_specs=[
            pl.BlockSpec((s_len, w), lambda b: (b, 0)),
            pl.BlockSpec((s_len, LANES), lambda b: (b, OF_FL // LANES)),
            pl.BlockSpec((1, LANES), const),
            pl.BlockSpec((3 * LANES, w), const),
            pl.BlockSpec((1, w), const),
        ],
        out_specs=pl.BlockSpec((s_len, w), lambda b: (b, 0)),
        out_shape=jax.ShapeDtypeStruct((n, w), BF16),
        scratch_shapes=[pltpu.VMEM((s_len, LANES), F32)],
        compiler_params=_cparams(("parallel",)),
        name="foxprep",
    )(ob, of, bias, place, ones)


def _softmax_attn_kernel(q_ref, k_ref, v_ref, o_ref, *, tq, pair_sum):
    qi = pl.program_id(1)
    row = lax.broadcasted_iota(jnp.int32, (tq, tq), 0)
    col = lax.broadcasted_iota(jnp.int32, (tq, tq), 1)
    causal = col <= row
    slots = [slice(h * SLOT, (h + 1) * SLOT) for h in range(N_HEADS)]
    qs = [q_ref[:, hs] for hs in slots]

    ones = jnp.ones((tq, SLOT), BF16)

    def block(j, carry, masked):
        rows = pl.ds(pl.multiple_of(j * tq, tq), tq)
        ss = [_dot_nt(qs[h], k_ref[rows, slots[h]]) for h in range(N_HEADS)]
        ms, alphas, ps = [], [], []
        for h in range(N_HEADS):
            s = jnp.where(causal, ss[h], -1e30) if masked else ss[h]
            m_new = jnp.maximum(carry[h][0], jnp.max(s, axis=1, keepdims=True))
            ms.append(m_new)
            alphas.append(jnp.exp(carry[h][0] - m_new))
            ps.append(jnp.exp(s - m_new).astype(BF16))
        if pair_sum:
            pvs = []
            for pr in range(N_HEADS // 2):
                vx = jnp.concatenate([v_ref[rows, slots[pr]], ones], axis=1)
                both = _dot(jnp.concatenate([ps[2 * pr], ps[2 * pr + 1]], axis=0), vx)
                pvs += [both[:tq], both[tq:]]
        else:
            pvs = [_dot(ps[h], jnp.concatenate([v_ref[rows, slots[h]], ones], axis=1)) for h in range(N_HEADS)]
        return tuple((ms[h], alphas[h] * carry[h][1] + pvs[h][:, SLOT:], alphas[h] * carry[h][2] + pvs[h][:, :SLOT])
                     for h in range(N_HEADS))

    init = tuple((jnp.full((tq, 1), -1e30, F32), jnp.zeros((tq, SLOT), F32), jnp.zeros((tq, SLOT), F32))
                 for _ in range(N_HEADS))
    carry = lax.fori_loop(0, qi, lambda j, c: block(j, c, False), init)
    outs = [acc / l for _, l, acc in block(qi, carry, True)]
    if pair_sum:
        lane = lax.broadcasted_iota(jnp.int32, (tq, SLOT), 1)
        for pr in range(N_HEADS // 2):
            o_ref[:, slots[pr]] = jnp.where(lane < HEAD_DIM, outs[2 * pr], outs[2 * pr + 1])
    else:
        for h in range(N_HEADS):
            o_ref[:, slots[h]] = outs[h]


def _softmax_attn(q_arr, q_col, k_arr, k_col, v_arr, v_col, batch, s_len, pair_sum):
    n = q_arr.shape[0]
    tq = TQ_SOFTMAX
    nq = s_len // tq
    vw = GW // 2 if pair_sum else GW
    return pl.pallas_call(
        functools.partial(_softmax_attn_kernel, tq=tq, pair_sum=pair_sum),
        grid=(batch, nq),
        in_specs=[
            pl.BlockSpec((tq, GW), lambda b, i: (b * nq + i, q_col)),
            pl.BlockSpec((s_len, GW), lambda b, i: (b, k_col)),
            pl.BlockSpec((s_len, vw), lambda b, i: (b, v_col)),
        ],
        out_specs=pl.BlockSpec((tq, vw), lambda b, i: (b * nq + i, 0)),
        out_shape=jax.ShapeDtypeStruct((n, vw), F32),
        compiler_params=_cparams(("parallel", "arbitrary")),
        name="fox_attn" if pair_sum else "mla_attn",
    )(q_arr, k_arr, v_arr)


def _sb_attn_kernel(q_ref, k_ref, v_ref, o_ref, *, tq, tk):
    qi = pl.program_id(1)
    nz = tq // tk
    row = lax.broadcasted_iota(jnp.int32, (tq, tk), 0)
    col = lax.broadcasted_iota(jnp.int32, (tq, tk), 1)
    ur = lax.broadcasted_iota(jnp.int32, (tk, 2 * tk), 0)
    uc = lax.broadcasted_iota(jnp.int32, (tk, 2 * tk), 1)
    uo = jnp.where((uc >= tk) | (ur > uc), 1.0, 0.0).astype(BF16)
    uo2 = jnp.concatenate([uo, uo], axis=0)
    slots = [slice(h * SLOT, (h + 1) * SLOT) for h in range(N_HEADS)]
    qs = [q_ref[:, hs] for hs in slots]

    lane = lax.broadcasted_iota(jnp.int32, (tq, SLOT), 1)

    def blocks(js, carry, zones):
        cs, accs = list(carry[0]), list(carry[1])
        rows = [pl.ds(pl.multiple_of(j * tk, tk), tk) for j in js]
        stricts = [None if zn is None else (col + zn * tk) < row for zn in zones]
        lbs, hls = [], []
        for b in range(len(js)):
            for h in range(N_HEADS):
                z = _dot_nt(qs[h], k_ref[rows[b], slots[h]])
                lb = jnp.minimum(z, 0.0) - jnp.log(1.0 + jnp.exp(-jnp.abs(z)))
                lom = lb - z
                if stricts[b] is not None:
                    lom = jnp.where(stricts[b], lom, 0.0)
                hi = lom.astype(BF16)
                lo = (lom - hi.astype(F32)).astype(BF16)
                lbs.append(lb)
                hls.append(jnp.concatenate([hi, lo], axis=1))
        r = _dot(jnp.concatenate(hls, axis=0), uo2)
        for b in range(len(js)):
            ws = []
            for h in range(N_HEADS):
                rh = r[(b * N_HEADS + h) * tq:(b * N_HEADS + h + 1) * tq]
                w = jnp.exp(lbs[b * N_HEADS + h] + rh[:, :tk] + cs[h])
                if stricts[b] is not None:
                    w = jnp.where(stricts[b], w, 0.0)
                ws.append(w.astype(BF16))
                cs[h] = cs[h] + rh[:, tk:]
            for pr in range(N_HEADS // 2):
                both = _dot(jnp.concatenate([ws[2 * pr], ws[2 * pr + 1]], axis=0), v_ref[rows[b], slots[pr]])
                accs[pr] = accs[pr] + jnp.where(lane < HEAD_DIM, both[:tq], both[tq:])
        return tuple(cs), tuple(accs)

    carry = (tuple(jnp.zeros((tq, tk), F32) for _ in range(N_HEADS)),
             tuple(jnp.zeros((tq, SLOT), F32) for _ in range(N_HEADS // 2)))
    zone = list(reversed(range(nz)))
    carry = blocks([qi * nz + zi for zi in zone], carry, zone)
    n_full = qi * nz
    per = 2 * nz

    def sweep(first, count):
        return lambda jj, cr: blocks([first - jj * count - i for i in range(count)], cr, [None] * count)

    n_main = n_full // per
    carry = lax.fori_loop(0, n_main, sweep(n_full - 1, per), carry)
    carry = lax.fori_loop(0, (n_full - n_main * per) // nz, sweep(n_full - 1 - n_main * per, nz), carry)
    for pr in range(N_HEADS // 2):
        o_ref[:, slots[pr]] = carry[1][pr]


def _sb_attn(ob, batch, s_len):
    n = ob.shape[0]
    tq, tk = TQ_SB, TK_SB
    nq = s_len // tq
    vw = GW // 2
    return pl.pallas_call(
        functools.partial(_sb_attn_kernel, tq=tq, tk=tk),
        grid=(batch, nq),
        in_specs=[
            pl.BlockSpec((tq, GW), lambda b, i: (b * nq + i, OB_SQ // GW)),
            pl.BlockSpec((s_len, GW), lambda b, i: (b, OB_SK // GW)),
            pl.BlockSpec((s_len, vw), lambda b, i: (b, OB_SV // vw)),
        ],
        out_specs=pl.BlockSpec((tq, vw), lambda b, i: (b * nq + i, 0)),
        out_shape=jax.ShapeDtypeStruct((n, vw), F32),
        compiler_params=_cparams(("parallel", "arbitrary")),
        name="sb_attn",
    )(ob, ob, ob)


def _rope_tables_kernel(pos_ref, invf_ref, cs_ref, sn_ref):
    ang = pos_ref[...].astype(F32) * invf_ref[...]
    cs_ref[...] = jnp.cos(ang)
    sn_ref[...] = jnp.sin(ang)


def _rope_tables(pos, invf):
    n = pos.shape[0]
    tm = TM_PROJ
    return pl.pallas_call(
        _rope_tables_kernel,
        grid=(n // tm,),
        in_specs=[pl.BlockSpec((tm, 1), lambda i: (i, 0)), pl.BlockSpec((1, LANES), lambda i: (0, 0))],
        out_specs=[pl.BlockSpec((tm, LANES), lambda i: (i, 0))] * 2,
        out_shape=[jax.ShapeDtypeStruct((n, LANES), F32)] * 2,
        compiler_params=_cparams(("parallel",)),
        name="rope_tables",
    )(pos, invf)


def _mla_proj_kernel(cq_ref, ckv_ref, kra_ref, krb_ref, cs_ref, sn_ref, gq_ref, gkv_ref,
                     wqa_ref, wqb_ref, wk_ref, wv_ref, q_ref, k_ref, v_ref, *, scale):
    cs, sn = cs_ref[...], sn_ref[...]
    cqn = _rms(cq_ref[...], gq_ref[...]).astype(BF16)
    ckvn = _rms(ckv_ref[...], gkv_ref[...]).astype(BF16)
    qa, qb = _dot(cqn, wqa_ref[...]), _dot(cqn, wqb_ref[...])
    kn = _dot(ckvn, wk_ref[...])
    v_ref[...] = _dot(ckvn, wv_ref[...]).astype(BF16)
    kr = kra_ref[...] * cs + krb_ref[...] * sn
    for h in range(N_HEADS):
        hs = slice(h * SLOT, (h + 1) * SLOT)
        q_ref[:, hs] = ((qa[:, hs] * cs + qb[:, hs] * sn) * scale).astype(BF16)
        k_ref[:, hs] = (kn[:, hs] + kr).astype(BF16)


def _mla_proj(of, cs, sn, gq, gkv, wqa, wqb, wk, wv):
    n = of.shape[0]
    tm = TM_PROJ
    w = N_HEADS * SLOT
    const = lambda i: (0, 0)
    return pl.pallas_call(
        functools.partial(_mla_proj_kernel, scale=(MLA_NOPE + MLA_ROPE) ** -0.5),
        grid=(n // tm,),
        in_specs=[
            pl.BlockSpec((tm, MLA_Q_RANK), lambda i: (i, OF_CQ // MLA_Q_RANK)),
            pl.BlockSpec((tm, LANES), lambda i: (i, OF_CKV // LANES)),
            pl.BlockSpec((tm, LANES), lambda i: (i, OF_KRA // LANES)),
            pl.BlockSpec((tm, LANES), lambda i: (i, OF_KRB // LANES)),
            pl.BlockSpec((tm, LANES), lambda i: (i, 0)),
            pl.BlockSpec((tm, LANES), lambda i: (i, 0)),
            pl.BlockSpec((1, MLA_Q_RANK), const),
            pl.BlockSpec((1, MLA_KV_RANK), const),
            pl.BlockSpec((MLA_Q_RANK, w), const),
            pl.BlockSpec((MLA_Q_RANK, w), const),
            pl.BlockSpec((MLA_KV_RANK, w), const),
            pl.BlockSpec((MLA_KV_RANK, w), const),
        ],
        out_specs=[pl.BlockSpec((tm, w), lambda i: (i, 0))] * 3,
        out_shape=[jax.ShapeDtypeStruct((n, w), BF16)] * 3,
        compiler_params=_cparams(("parallel",)),
        name="mla_proj",
    )(of, of, of, of, cs, sn, gq, gkv, wqa, wqb, wk, wv)


def _outproj_kernel(yf_ref, ys_ref, ym_ref, g_ref, wo_ref, h_ref, o_ref):
    gw = yf_ref.shape[1]
    nf = _rms(yf_ref[...], g_ref[:, 0:gw]).astype(BF16)
    ns = _rms(ys_ref[...], g_ref[:, gw:2 * gw]).astype(BF16)
    nm = _rms(ym_ref[...], g_ref[:, 2 * gw:]).astype(BF16)
    acc = _dot(nf, wo_ref[0:gw, :]) + _dot(ns, wo_ref[gw:2 * gw, :]) + _dot(nm, wo_ref[2 * gw:, :])
    o_ref[...] = h_ref[...] + acc


def _outproj(yf, ys, ym, g, wo, h):
    n = h.shape[0]
    tm = TM_PROJ
    const = lambda i: (0, 0)
    return pl.pallas_call(
        _outproj_kernel,
        grid=(n // tm,),
        in_specs=[
            pl.BlockSpec((tm, yf.shape[1]), lambda i: (i, 0)),
            pl.BlockSpec((tm, ys.shape[1]), lambda i: (i, 0)),
            pl.BlockSpec((tm, ym.shape[1]), lambda i: (i, 0)),
            pl.BlockSpec((1, D_MODEL), const),
            pl.BlockSpec((D_MODEL, D_MODEL), const),
            pl.BlockSpec((tm, D_MODEL), lambda i: (i, 0)),
        ],
        out_specs=pl.BlockSpec((tm, D_MODEL), lambda i: (i, 0)),
        out_shape=jax.ShapeDtypeStruct((n, D_MODEL), F32),
        compiler_params=_cparams(("parallel",)),
        name="outproj",
    )(yf, ys, ym, g, wo, h)


def _batcher_pairs(n):
    def merge(lo, hi, r):
        step = r * 2
        if step < hi - lo:
            yield from merge(lo, hi, step)
            yield from merge(lo + r, hi, step)
            yield from ((i, i + r) for i in range(lo + r, hi - r, step))
        else:
            yield (lo, lo + r)

    def sort(lo, hi):
        if hi - lo >= 1:
            mid = lo + (hi - lo) // 2
            yield from sort(lo, mid)
            yield from sort(mid + 1, hi)
            yield from merge(lo, hi, 1)

    return tuple(sort(0, n - 1))


def _compare_exchange(v, pairs):
    v = list(v)
    for i, j in pairs:
        v[i], v[j] = jnp.maximum(v[i], v[j]), jnp.minimum(v[i], v[j])
    return v


def _bitonic_pairs(n):
    return tuple((i, i + d) for d in (n >> k for k in range(1, n.bit_length())) for i in range(n) if not i & d)


def _top16_rows(slabs):
    n = PEER_TOPK
    v = _compare_exchange(slabs, _batcher_pairs(len(slabs)))
    for shift in (4, 2, 1):
        if len(v) < n:
            v = v + [pltpu.roll(x, shift, axis=0) for x in reversed(v)]
        else:
            v = [jnp.maximum(v[i], pltpu.roll(v[n - 1 - i], shift, axis=0)) for i in range(n)]
        v = _compare_exchange(v, _bitonic_pairs(n))
    return v


def _peer_kernel(*refs, tile, eb):
    (h_ref, g_ref, wq_ref, sk_ref, u_ref, vt_ref,
     o_ref, xnt_ref, e2_ref, e1s_ref, g16_ref, tv_ref, cand_ref, candn_ref,
     pre_ref, act_ref, acc_ref) = refs
    s = pl.program_id(1)

    @pl.when(s == 0)
    def _():
        xn = _rms(h_ref[...], g_ref[...])
        xnt = xn.T
        xnt_ref[...] = xnt.astype(BF16)
        qt = _dot(wq_ref[...], xnt_ref[...]).astype(BF16)
        cand_ref[N_CAND:, :] = jnp.full((N_CAND_PAD - N_CAND, tile), -1.0, F32)
        candn_ref[N_CAND:, :] = jnp.full((N_CAND_PAD - N_CAND, tile), -1.0, F32)
        for hd in range(PEER_HEADS):
            es = []
            for p in range(2):
                base = (hd * 2 + p) * PEER_HALF
                sc = _dot(sk_ref[hd * 2 + p], qt[base:base + PEER_HALF, :])
                e = jnp.exp(sc - jnp.max(sc, axis=0, keepdims=True))
                es.append(e)
                top = _top16_rows([e[8 * i:8 * i + 8, :] for i in range(PEER_NKEYS // 8)])
                for r in range(PEER_TOPK):
                    tv_ref[p, r:r + 1, :] = top[r][0:1, :]
            for i, (a, b) in enumerate(PEER_PAIRS):
                cand_ref[i:i + 1, :] = tv_ref[0, a:a + 1, :] * tv_ref[1, b:b + 1, :]
            cand = cand_ref[...]
            pad = jnp.full((8, tile), -1.0, F32)
            cslabs = [cand[8 * i:8 * i + 8, :] for i in range(N_CAND_PAD // 8)] + [pad] * (8 - N_CAND_PAD // 8)
            tau = _top16_rows(cslabs)[PEER_TOPK - 1][0:1, :]
            sel = cand >= tau
            inv_z = 0.5 / jnp.sum(jnp.where(sel, cand, 0.0), axis=0, keepdims=True)
            e1s_ref[hd] = es[0] * inv_z
            for lg in range(tile // LANES):
                e2_ref[hd, lg] = es[1][:, lg * LANES:(lg + 1) * LANES]
            t1s = tv_ref[0] * inv_z
            for i, (a, b) in enumerate(PEER_PAIRS):
                candn_ref[i:i + 1, :] = t1s[a:a + 1, :] * tv_ref[1, b:b + 1, :]
            g16_ref[hd] = jnp.min(jnp.where(sel, candn_ref[...], jnp.inf), axis=0, keepdims=True)
        acc_ref[...] = jnp.zeros_like(acc_ref)

    npc = eb // PEER_NKEYS
    nlg = tile // LANES
    rc = PEER_NKEYS // 2
    erf_scale = 1.0 / math.sqrt(2.0)

    pre = _dot(u_ref[...], xnt_ref[...])
    for lg in range(nlg):
        pre_ref[lg] = pre[:, lg * LANES:(lg + 1) * LANES]
    for a in range(npc):
        e1full = [e1s_ref[hd, pl.ds(s * npc + a, 1), :] for hd in range(PEER_HEADS)]
        for lg in range(nlg):
            cols = slice(lg * LANES, (lg + 1) * LANES)
            e1rows = [row[:, cols] for row in e1full]
            g16rows = [g16_ref[hd, :, cols] for hd in range(PEER_HEADS)]
            for r0 in range(0, PEER_NKEYS, rc):
                gate = None
                for hd in range(PEER_HEADS):
                    val = e2_ref[hd, lg, r0:r0 + rc, :] * e1rows[hd]
                    contrib = jnp.where(val >= g16rows[hd], val, 0.0)
                    gate = contrib if gate is None else gate + contrib
                rows = slice(a * PEER_NKEYS + r0, a * PEER_NKEYS + r0 + rc)
                pa = pre_ref[lg, rows, :]
                act_ref[lg, rows, :] = (gate * (pa * (1.0 + lax.erf(pa * erf_scale)))).astype(BF16)
    act = jnp.concatenate([act_ref[lg] for lg in range(nlg)], axis=1)
    acc_ref[...] += _dot(vt_ref[...], act)

    @pl.when(s == pl.num_programs(1) - 1)
    def _():
        o_ref[...] = h_ref[...] + acc_ref[...].T


def _peer(h, g, wqt, sk, u, v):
    n = h.shape[0]
    n_exp = u.shape[0]
    tile, eb = T_PEER, EB_PEER
    ns = n_exp // eb
    dq = wqt.shape[0]
    vt = v.reshape(ns, eb, D_MODEL).transpose(0, 2, 1).astype(BF16)
    return pl.pallas_call(
        functools.partial(_peer_kernel, tile=tile, eb=eb),
        grid=(n // tile, ns),
        in_specs=[
            pl.BlockSpec((tile, D_MODEL), lambda i, s: (i, 0)),
            pl.BlockSpec((1, D_MODEL), lambda i, s: (0, 0)),
            pl.BlockSpec((dq, D_MODEL), lambda i, s: (0, 0)),
            pl.BlockSpec((2 * PEER_HEADS, PEER_NKEYS, PEER_HALF), lambda i, s: (0, 0, 0)),
            pl.BlockSpec((eb, D_MODEL), lambda i, s: (s, 0)),
            pl.BlockSpec((None, D_MODEL, eb), lambda i, s: (s, 0, 0)),
        ],
        out_specs=pl.BlockSpec((tile, D_MODEL), lambda i, s: (i, 0)),
        out_shape=jax.ShapeDtypeStruct((n, D_MODEL), F32),
        scratch_shapes=[
            pltpu.VMEM((D_MODEL, tile), BF16),
            pltpu.VMEM((PEER_HEADS, tile // LANES, PEER_NKEYS, LANES), F32),
            pltpu.VMEM((PEER_HEADS, PEER_NKEYS, tile), F32),
            pltpu.VMEM((PEER_HEADS, 1, tile), F32),
            pltpu.VMEM((2, PEER_TOPK, tile), F32),
            pltpu.VMEM((N_CAND_PAD, tile), F32),
            pltpu.VMEM((N_CAND_PAD, tile), F32),
            pltpu.VMEM((tile // LANES, eb, LANES), F32),
            pltpu.VMEM((tile // LANES, eb, LANES), BF16),
            pltpu.VMEM((D_MODEL, tile), F32),
        ],
        compiler_params=_cparams(("parallel", "arbitrary")),
        name="peer",
    )(h, g, wqt, sk, u, vt)


def _ple_kernel(h_ref, p_ref, g_ref, wg_ref, wp_ref, fg_ref, o_ref, *, final):
    h = h_ref[...]
    gate = jax.nn.sigmoid(_dot(_rms(h, g_ref[...]).astype(BF16), wg_ref[...]))
    out = h + gate * _dot(p_ref[...].astype(BF16), wp_ref[...])
    o_ref[...] = _rms(out, fg_ref[...]) if final else out


def _ple(h, p, g, wg, wp, fg, final):
    n = h.shape[0]
    tm = TM_PROJ
    const = lambda i: (0, 0)
    return pl.pallas_call(
        functools.partial(_ple_kernel, final=final),
        grid=(n // tm,),
        in_specs=[
            pl.BlockSpec((tm, D_MODEL), lambda i: (i, 0)),
            pl.BlockSpec((tm, PLE_DIM), lambda i: (i, 0)),
            pl.BlockSpec((1, D_MODEL), const),
            pl.BlockSpec((D_MODEL, D_MODEL), const),
            pl.BlockSpec((PLE_DIM, D_MODEL), const),
            pl.BlockSpec((1, D_MODEL), const),
        ],
        out_specs=pl.BlockSpec((tm, D_MODEL), lambda i: (i, 0)),
        out_shape=jax.ShapeDtypeStruct((n, D_MODEL), F32),
        compiler_params=_cparams(("parallel",)),
        name="ple",
    )(h, p, g, wg, wp, fg)


def _pad_heads(w, width=SLOT):
    k = w.shape[0]
    w = w.reshape(k, N_HEADS, -1)
    return jnp.pad(w, ((0, 0), (0, 0), (0, width - w.shape[2]))).reshape(k, N_HEADS * width)


def _rot_partner(w_rot):
    half = MLA_ROPE // 2
    return jnp.concatenate([-w_rot[..., half:], w_rot[..., :half]], axis=-1)


def _layer_weights(w_in, w_uq, w_ukv):
    offs = [int(o) for o in np.cumsum(IN_SPLITS)[:-1]]
    fq, fk, fv, fl, sq, sk, sv, cq, ckv, kr = jnp.split(w_in, offs, axis=1)
    att_scale = HEAD_DIM ** -0.5
    wb = jnp.concatenate([_pad_heads(fq * att_scale), _pad_heads(fk),
                          _pad_heads(sq * att_scale), _pad_heads(sk), fv, sv], axis=1).astype(BF16)
    z = lambda c: jnp.zeros((D_MODEL, c), F32)
    kra = jnp.concatenate([z(MLA_NOPE), kr, z(SLOT - MLA_NOPE - MLA_ROPE)], axis=1)
    krb = jnp.concatenate([z(MLA_NOPE), _rot_partner(kr), z(SLOT - MLA_NOPE - MLA_ROPE)], axis=1)
    wf = jnp.concatenate([cq, ckv, kra, krb, fl, z(LANES - N_HEADS)], axis=1).astype(BF16)
    uq = w_uq.reshape(MLA_Q_RANK, N_HEADS, MLA_NOPE + MLA_ROPE)
    zq = jnp.zeros((MLA_Q_RANK, N_HEADS, SLOT - MLA_NOPE - MLA_ROPE), F32)
    wqa = jnp.concatenate([uq, zq], axis=2).reshape(MLA_Q_RANK, N_HEADS * SLOT).astype(BF16)
    wqb = jnp.concatenate([jnp.zeros_like(uq[..., :MLA_NOPE]), _rot_partner(uq[..., MLA_NOPE:]), zq],
                          axis=2).reshape(MLA_Q_RANK, N_HEADS * SLOT).astype(BF16)
    ukv = w_ukv.reshape(MLA_KV_RANK, N_HEADS, MLA_NOPE + MLA_V)
    wk = jnp.pad(ukv[..., :MLA_NOPE], ((0, 0), (0, 0), (0, SLOT - MLA_NOPE))).reshape(MLA_KV_RANK, -1).astype(BF16)
    wv = ukv[..., MLA_NOPE:].reshape(MLA_KV_RANK, -1).astype(BF16)
    return wb, wf, wqa, wqb, wk, wv


def _rope_lane_freqs():
    half = MLA_ROPE // 2
    inv_freq = ROPE_THETA ** (-jnp.arange(half, dtype=F32) / half)
    zeros = lambda c: jnp.zeros((c,), F32)
    return jnp.concatenate([zeros(MLA_NOPE), inv_freq, inv_freq, zeros(SLOT - MLA_NOPE - MLA_ROPE)]).reshape(1, SLOT)


def kernel(x, p, positions, norm_mix_g, w_in, b_forget, mla_q_norm_g, w_uq, mla_kv_norm_g, w_ukv, mix_out_norm_g, w_o, norm_ffn_g, peer_w_query, peer_sub_keys, peer_u, peer_v, w_ple, ple_norm_g, w_ple_gate, final_norm_g):
    batch, s_len, d = x.shape
    depth = p.shape[0]
    n = batch * s_len
    h = x.reshape(n, d)
    pos = positions.reshape(n, 1)
    rope_cs, rope_sn = _rope_tables(pos, _rope_lane_freqs())
    row = lambda v: v.reshape(1, -1)
    for i in range(depth):
        wb, wf, wqa, wqb, wk, wv = _layer_weights(w_in[i], w_uq[i], w_ukv[i])
        ob, of = _inproj(h, row(norm_mix_g[i]), wb, wf)
        bias = jnp.pad(b_forget[i], (0, LANES - N_HEADS)).reshape(1, LANES)
        fqk = _foxprep(ob, of, bias, batch, s_len)
        y_fox = _softmax_attn(fqk, 0, fqk, 1, ob, OB_FV // (GW // 2), batch, s_len, True)
        y_sb = _sb_attn(ob, batch, s_len)
        mq, mk, mv = _mla_proj(of, rope_cs, rope_sn, row(mla_q_norm_g[i]), row(mla_kv_norm_g[i]), wqa, wqb, wk, wv)
        y_mla = _softmax_attn(mq, 0, mk, 0, mv, 0, batch, s_len, False)
        h = _outproj(y_fox, y_sb, y_mla, row(mix_out_norm_g[i]), w_o[i].astype(BF16), h)
        sk = peer_sub_keys[i].reshape(2 * PEER_HEADS, PEER_NKEYS, PEER_HALF).astype(BF16)
        h = _peer(h, row(norm_ffn_g[i]), peer_w_query[i].T.astype(BF16), sk, peer_u[i].astype(BF16), peer_v[i])
        h = _ple(h, p[i].reshape(n, PLE_DIM), row(ple_norm_g[i]), w_ple_gate[i].astype(BF16),
                 w_ple[i].astype(BF16), row(final_norm_g), i == depth - 1)
    return h.reshape(batch, s_len, d)
```

```python
import functools
import math

import numpy as np
import jax
import jax.numpy as jnp
from jax import lax
from jax.experimental import pallas as pl
from jax.experimental.pallas import tpu as pltpu

F32 = jnp.float32
BF16 = jnp.bfloat16

D_MODEL = 1024
HEAD_DIM = 64
N_HEADS = 4
MLA_Q_RANK = 256
MLA_KV_RANK = 128
MLA_NOPE = 64
MLA_ROPE = 32
MLA_V = 128
ROPE_THETA = 10000.0
IN_SPLITS = (256, 256, 256, 4, 256, 256, 256, MLA_Q_RANK, MLA_KV_RANK, MLA_ROPE)
PEER_HEADS = 8
PEER_NKEYS = 128
PEER_HALF = 64
PEER_TOPK = 16
PLE_DIM = 256
EPS = 1e-6

LANES = 128
SLOT = LANES
VMEM_LIMIT = 56 * 1024 * 1024

TM_PROJ = 1024
TQ_SOFTMAX = 512
TQ_SB = 256
TK_SB = 128
T_PEER = 512
EB_PEER = 1024

OB_FQ, OB_FK, OB_SQ, OB_SK, OB_FV, OB_SV, OB_W = 0, 512, 1024, 1280, 1536, 1792, 2048
GW = N_HEADS * SLOT
OF_CQ, OF_CKV, OF_KRA, OF_KRB, OF_FL, OF_W = 0, 256, 384, 512, 640, 768

PEER_PAIRS = tuple((a, b) for a in range(PEER_TOPK) for b in range(PEER_TOPK) if (a + 1) * (b + 1) <= PEER_TOPK)
N_CAND = len(PEER_PAIRS)
N_CAND_PAD = 56


def _cparams(sem):
    return pltpu.CompilerParams(dimension_semantics=sem, vmem_limit_bytes=VMEM_LIMIT)


def _rms(x, g):
    return x * lax.rsqrt(jnp.mean(x * x, axis=-1, keepdims=True) + EPS) * g


def _dot(a, b):
    return jnp.dot(a, b, preferred_element_type=F32)


def _dot_nt(a, b):
    return lax.dot_general(a, b, (((1,), (1,)), ((), ())), preferred_element_type=F32)


def _inproj_kernel(x_ref, g_ref, wb_ref, wf_ref, ob_ref, of_ref):
    xb = _rms(x_ref[...], g_ref[...]).astype(BF16)
    ob_ref[...] = _dot(xb, wb_ref[...]).astype(BF16)
    of_ref[...] = _dot(xb, wf_ref[...])


def _inproj(h, g, wb, wf):
    n = h.shape[0]
    tm = TM_PROJ
    return pl.pallas_call(
        _inproj_kernel,
        grid=(n // tm,),
        in_specs=[
            pl.BlockSpec((tm, D_MODEL), lambda i: (i, 0)),
            pl.BlockSpec((1, D_MODEL), lambda i: (0, 0)),
            pl.BlockSpec((D_MODEL, OB_W), lambda i: (0, 0)),
            pl.BlockSpec((D_MODEL, OF_W), lambda i: (0, 0)),
        ],
        out_specs=[
            pl.BlockSpec((tm, OB_W), lambda i: (i, 0)),
            pl.BlockSpec((tm, OF_W), lambda i: (i, 0)),
        ],
        out_shape=[jax.ShapeDtypeStruct((n, OB_W), BF16), jax.ShapeDtypeStruct((n, OF_W), F32)],
        compiler_params=_cparams(("parallel",)),
        name="inproj",
    )(h, g, wb, wf)


def _split3(x):
    hi = x.astype(BF16)
    r = x - hi.astype(F32)
    mid = r.astype(BF16)
    lo = (r - mid.astype(F32)).astype(BF16)
    return hi, mid, lo


FOX_AUG_LANE = HEAD_DIM


def _fox_placement():
    place = np.zeros((3 * LANES, 2 * GW), np.float32)
    ones = np.zeros((1, 2 * GW), np.float32)
    for h in range(N_HEADS):
        for t in range(3):
            place[t * LANES + h, h * SLOT + FOX_AUG_LANE + t] = 1.0
            place[t * LANES + h, GW + h * SLOT + FOX_AUG_LANE + 3 + t] = -1.0
            ones[0, h * SLOT + FOX_AUG_LANE + 3 + t] = 1.0
            ones[0, GW + h * SLOT + FOX_AUG_LANE + t] = 1.0
    return jnp.asarray(place, BF16), jnp.asarray(ones, F32)


def _foxprep_kernel(qk_ref, fl_ref, b_ref, place_ref, ones_ref, o_ref, f_ref, *, rb):
    s_len = fl_ref.shape[0]
    x = fl_ref[...] + b_ref[...]
    logf = jnp.minimum(x, 0.0) - jnp.log(1.0 + jnp.exp(-jnp.abs(x)))
    row = lax.broadcasted_iota(jnp.int32, (rb, rb), 0)
    col = lax.broadcasted_iota(jnp.int32, (rb, rb), 1)
    tri = jnp.where(col <= row, 1.0, 0.0).astype(BF16)
    carry = jnp.zeros((1, LANES), F32)
    for r in range(s_len // rb):
        hi, mid, lo = _split3(logf[r * rb:(r + 1) * rb, :])
        fb = _dot(tri, hi) + _dot(tri, mid) + _dot(tri, lo) + carry
        f_ref[r * rb:(r + 1) * rb, :] = fb
        carry = fb[rb - 1:rb, :]
    terms = jnp.concatenate(_split3(f_ref[...]), axis=1)
    aug = _dot(terms, place_ref[...]) + ones_ref[...]
    o_ref[...] = (qk_ref[...].astype(F32) + aug).astype(BF16)


def _foxprep(ob, of, bias, batch, s_len):
    n = ob.shape[0]
    w = 2 * GW
    place, ones = _fox_placement()
    const = lambda b: (0, 0)
    return pl.pallas_call(
        functools.partial(_foxprep_kernel, rb=256),
        grid=(batch,),
        in_specs=[
            pl.BlockSpec((s_len, w), lambda b: (b, 0)),
            pl.BlockSpec((s_len, LANES), lambda b: (b, OF_FL // LANES)),
            pl.BlockSpec((1, LANES), const),
            pl.BlockSpec((3 * LANES, w), const),
            pl.BlockSpec((1, w), const),
        ],
        out_specs=pl.BlockSpec((s_len, w), lambda b: (b, 0)),
        out_shape=jax.ShapeDtypeStruct((n, w), BF16),
        scratch_shapes=[pltpu.VMEM((s_len, LANES), F32)],
        compiler_params=_cparams(("parallel",)),
        name="foxprep",
    )(ob, of, bias, place, ones)


def _softmax_attn_kernel(q_ref, k_ref, v_ref, o_ref, *, tq, pair_sum):
    qi = pl.program_id(1)
    row = lax.broadcasted_iota(jnp.int32, (tq, tq), 0)
    col = lax.broadcasted_iota(jnp.int32, (tq, tq), 1)
    causal = col <= row
    slots = [slice(h * SLOT, (h + 1) * SLOT) for h in range(N_HEADS)]
    qs = [q_ref[:, hs] for hs in slots]

    ones = jnp.ones((tq, SLOT), BF16)

    def block(j, carry, masked):
        rows = pl.ds(pl.multiple_of(j * tq, tq), tq)
        ss = [_dot_nt(qs[h], k_ref[rows, slots[h]]) for h in range(N_HEADS)]
        ms, alphas, ps = [], [], []
        for h in range(N_HEADS):
            s = jnp.where(causal, ss[h], -1e30) if masked else ss[h]
            m_new = jnp.maximum(carry[h][0], jnp.max(s, axis=1, keepdims=True))
            ms.append(m_new)
            alphas.append(jnp.exp(carry[h][0] - m_new))
            ps.append(jnp.exp(s - m_new).astype(BF16))
        if pair_sum:
            pvs = []
            for pr in range(N_HEADS // 2):
                vx = jnp.concatenate([v_ref[rows, slots[pr]], ones], axis=1)
                both = _dot(jnp.concatenate([ps[2 * pr], ps[2 * pr + 1]], axis=0), vx)
                pvs += [both[:tq], both[tq:]]
        else:
            pvs = [_dot(ps[h], jnp.concatenate([v_ref[rows, slots[h]], ones], axis=1)) for h in range(N_HEADS)]
        return tuple((ms[h], alphas[h] * carry[h][1] + pvs[h][:, SLOT:], alphas[h] * carry[h][2] + pvs[h][:, :SLOT])
                     for h in range(N_HEADS))

    init = tuple((jnp.full((tq, 1), -1e30, F32), jnp.zeros((tq, SLOT), F32), jnp.zeros((tq, SLOT), F32))
                 for _ in range(N_HEADS))
    carry = lax.fori_loop(0, qi, lambda j, c: block(j, c, False), init)
    outs = [acc / l for _, l, acc in block(qi, carry, True)]
    if pair_sum:
        lane = lax.broadcasted_iota(jnp.int32, (tq, SLOT), 1)
        for pr in range(N_HEADS // 2):
            o_ref[:, slots[pr]] = jnp.where(lane < HEAD_DIM, outs[2 * pr], outs[2 * pr + 1])
    else:
        for h in range(N_HEADS):
            o_ref[:, slots[h]] = outs[h]


def _softmax_attn(q_arr, q_col, k_arr, k_col, v_arr, v_col, batch, s_len, pair_sum):
    n = q_arr.shape[0]
    tq = TQ_SOFTMAX
    nq = s_len // tq
    vw = GW // 2 if pair_sum else GW
    return pl.pallas_call(
        functools.partial(_softmax_attn_kernel, tq=tq, pair_sum=pair_sum),
        grid=(batch, nq),
        in_specs=[
            pl.BlockSpec((tq, GW), lambda b, i: (b * nq + i, q_col)),
            pl.BlockSpec((s_len, GW), lambda b, i: (b, k_col)),
            pl.BlockSpec((s_len, vw), lambda b, i: (b, v_col)),
        ],
        out_specs=pl.BlockSpec((tq, vw), lambda b, i: (b * nq + i, 0)),
        out_shape=jax.ShapeDtypeStruct((n, vw), F32),
        compiler_params=_cparams(("parallel", "arbitrary")),
        name="fox_attn" if pair_sum else "mla_attn",
    )(q_arr, k_arr, v_arr)


def _sb_attn_kernel(q_ref, k_ref, v_ref, o_ref, *, tq, tk):
    qi = pl.program_id(1)
    nz = tq // tk
    row = lax.broadcasted_iota(jnp.int32, (tq, tk), 0)
    col = lax.broadcasted_iota(jnp.int32, (tq, tk), 1)
    ur = lax.broadcasted_iota(jnp.int32, (tk, 2 * tk), 0)
    uc = lax.broadcasted_iota(jnp.int32, (tk, 2 * tk), 1)
    uo = jnp.where((uc >= tk) | (ur > uc), 1.0, 0.0).astype(BF16)
    uo2 = jnp.concatenate([uo, uo], axis=0)
    slots = [slice(h * SLOT, (h + 1) * SLOT) for h in range(N_HEADS)]
    lane = lax.broadcasted_iota(jnp.int32, (tq, SLOT), 1)
    own = (lane < HEAD_DIM, lane >= HEAD_DIM)
    qs = [jnp.where(own[h % 2], q_ref[:, slots[h // 2]].astype(F32), 0.0).astype(BF16) for h in range(N_HEADS)]

    def blocks(js, carry, zones):
        cs, accs = list(carry[0]), list(carry[1])
        rows = [pl.ds(pl.multiple_of(j * tk, tk), tk) for j in js]
        stricts = [None if zn is None else (col + zn * tk) < row for zn in zones]
        lbs, hls = [], []
        for b in range(len(js)):
            for h in range(N_HEADS):
                z = _dot_nt(qs[h], k_ref[rows[b], slots[h // 2]])
                lb = jnp.minimum(z, 0.0) - jnp.log(1.0 + jnp.exp(-jnp.abs(z)))
                lom = lb - z
                if stricts[b] is not None:
                    lom = jnp.where(stricts[b], lom, 0.0)
                hi = lom.astype(BF16)
                lo = (lom - hi.astype(F32)).astype(BF16)
                lbs.append(lb)
                hls.append(jnp.concatenate([hi, lo], axis=1))
        r = _dot(jnp.concatenate(hls, axis=0), uo2)
        for b in range(len(js)):
            ws = []
            for h in range(N_HEADS):
                rh = r[(b * N_HEADS + h) * tq:(b * N_HEADS + h + 1) * tq]
                w = jnp.exp(lbs[b * N_HEADS + h] + rh[:, :tk] + cs[h])
                if stricts[b] is not None:
                    w = jnp.where(stricts[b], w, 0.0)
                ws.append(w.astype(BF16))
                cs[h] = cs[h] + rh[:, tk:]
            for pr in range(N_HEADS // 2):
                both = _dot(jnp.concatenate([ws[2 * pr], ws[2 * pr + 1]], axis=0), v_ref[rows[b], slots[pr]])
                accs[pr] = accs[pr] + jnp.where(lane < HEAD_DIM, both[:tq], both[tq:])
        return tuple(cs), tuple(accs)

    carry = (tuple(jnp.zeros((tq, tk), F32) for _ in range(N_HEADS)),
             tuple(jnp.zeros((tq, SLOT), F32) for _ in range(N_HEADS // 2)))
    zone = list(reversed(range(nz)))
    carry = blocks([qi * nz + zi for zi in zone], carry, zone)
    n_full = qi * nz
    per = 2 * nz

    def sweep(first, count):
        return lambda jj, cr: blocks([first - jj * count - i for i in range(count)], cr, [None] * count)

    n_main = n_full // per
    carry = lax.fori_loop(0, n_main, sweep(n_full - 1, per), carry)
    carry = lax.fori_loop(0, (n_full - n_main * per) // nz, sweep(n_full - 1 - n_main * per, nz), carry)
    for pr in range(N_HEADS // 2):
        o_ref[:, slots[pr]] = carry[1][pr]


def _sb_attn(ob, batch, s_len):
    n = ob.shape[0]
    tq, tk = TQ_SB, TK_SB
    nq = s_len // tq
    vw = GW // 2
    return pl.pallas_call(
        functools.partial(_sb_attn_kernel, tq=tq, tk=tk),
        grid=(batch, nq),
        in_specs=[
            pl.BlockSpec((tq, vw), lambda b, i: (b * nq + i, OB_SQ // vw)),
            pl.BlockSpec((s_len, vw), lambda b, i: (b, OB_SK // vw)),
            pl.BlockSpec((s_len, vw), lambda b, i: (b, OB_SV // vw)),
        ],
        out_specs=pl.BlockSpec((tq, vw), lambda b, i: (b * nq + i, 0)),
        out_shape=jax.ShapeDtypeStruct((n, vw), F32),
        compiler_params=_cparams(("parallel", "arbitrary")),
        name="sb_attn",
    )(ob, ob, ob)


def _mla_proj_kernel(cq_ref, ckv_ref, kra_ref, krb_ref, pos_ref, invf_ref, gq_ref, gkv_ref,
                     wqa_ref, wqb_ref, wk_ref, wv_ref, q_ref, k_ref, v_ref, *, scale):
    ang = pos_ref[...].astype(F32) * invf_ref[...]
    cs, sn = jnp.cos(ang), jnp.sin(ang)
    cqn = _rms(cq_ref[...], gq_ref[...]).astype(BF16)
    ckvn = _rms(ckv_ref[...], gkv_ref[...]).astype(BF16)
    qa, qb = _dot(cqn, wqa_ref[...]), _dot(cqn, wqb_ref[...])
    kn = _dot(ckvn, wk_ref[...])
    v_ref[...] = _dot(ckvn, wv_ref[...]).astype(BF16)
    kr = kra_ref[...] * cs + krb_ref[...] * sn
    for h in range(N_HEADS):
        hs = slice(h * SLOT, (h + 1) * SLOT)
        q_ref[:, hs] = ((qa[:, hs] * cs + qb[:, hs] * sn) * scale).astype(BF16)
        k_ref[:, hs] = (kn[:, hs] + kr).astype(BF16)


def _mla_proj(of, pos, invf, gq, gkv, wqa, wqb, wk, wv):
    n = of.shape[0]
    tm = TM_PROJ
    w = N_HEADS * SLOT
    const = lambda i: (0, 0)
    return pl.pallas_call(
        functools.partial(_mla_proj_kernel, scale=(MLA_NOPE + MLA_ROPE) ** -0.5),
        grid=(n // tm,),
        in_specs=[
            pl.BlockSpec((tm, MLA_Q_RANK), lambda i: (i, OF_CQ // MLA_Q_RANK)),
            pl.BlockSpec((tm, LANES), lambda i: (i, OF_CKV // LANES)),
            pl.BlockSpec((tm, LANES), lambda i: (i, OF_KRA // LANES)),
            pl.BlockSpec((tm, LANES), lambda i: (i, OF_KRB // LANES)),
            pl.BlockSpec((tm, 1), lambda i: (i, 0)),
            pl.BlockSpec((1, LANES), const),
            pl.BlockSpec((1, MLA_Q_RANK), const),
            pl.BlockSpec((1, MLA_KV_RANK), const),
            pl.BlockSpec((MLA_Q_RANK, w), const),
            pl.BlockSpec((MLA_Q_RANK, w), const),
            pl.BlockSpec((MLA_KV_RANK, w), const),
            pl.BlockSpec((MLA_KV_RANK, w), const),
        ],
        out_specs=[pl.BlockSpec((tm, w), lambda i: (i, 0))] * 3,
        out_shape=[jax.ShapeDtypeStruct((n, w), BF16)] * 3,
        compiler_params=_cparams(("parallel",)),
        name="mla_proj",
    )(of, of, of, of, pos, invf, gq, gkv, wqa, wqb, wk, wv)


def _outproj_kernel(yf_ref, ys_ref, ym_ref, g_ref, wo_ref, h_ref, o_ref):
    gw = yf_ref.shape[1]
    nf = _rms(yf_ref[...], g_ref[:, 0:gw]).astype(BF16)
    ns = _rms(ys_ref[...], g_ref[:, gw:2 * gw]).astype(BF16)
    nm = _rms(ym_ref[...], g_ref[:, 2 * gw:]).astype(BF16)
    acc = _dot(nf, wo_ref[0:gw, :]) + _dot(ns, wo_ref[gw:2 * gw, :]) + _dot(nm, wo_ref[2 * gw:, :])
    o_ref[...] = h_ref[...] + acc


def _outproj(yf, ys, ym, g, wo, h):
    n = h.shape[0]
    tm = TM_PROJ
    const = lambda i: (0, 0)
    return pl.pallas_call(
        _outproj_kernel,
        grid=(n // tm,),
        in_specs=[
            pl.BlockSpec((tm, yf.shape[1]), lambda i: (i, 0)),
            pl.BlockSpec((tm, ys.shape[1]), lambda i: (i, 0)),
            pl.BlockSpec((tm, ym.shape[1]), lambda i: (i, 0)),
            pl.BlockSpec((1, D_MODEL), const),
            pl.BlockSpec((D_MODEL, D_MODEL), const),
            pl.BlockSpec((tm, D_MODEL), lambda i: (i, 0)),
        ],
        out_specs=pl.BlockSpec((tm, D_MODEL), lambda i: (i, 0)),
        out_shape=jax.ShapeDtypeStruct((n, D_MODEL), F32),
        compiler_params=_cparams(("parallel",)),
        name="outproj",
    )(yf, ys, ym, g, wo, h)


def _batcher_pairs(n):
    def merge(lo, hi, r):
        step = r * 2
        if step < hi - lo:
            yield from merge(lo, hi, step)
            yield from merge(lo + r, hi, step)
            yield from ((i, i + r) for i in range(lo + r, hi - r, step))
        else:
            yield (lo, lo + r)

    def sort(lo, hi):
        if hi - lo >= 1:
            mid = lo + (hi - lo) // 2
            yield from sort(lo, mid)
            yield from sort(mid + 1, hi)
            yield from merge(lo, hi, 1)

    return tuple(sort(0, n - 1))


def _compare_exchange(v, pairs):
    v = list(v)
    for i, j in pairs:
        v[i], v[j] = jnp.maximum(v[i], v[j]), jnp.minimum(v[i], v[j])
    return v


def _bitonic_pairs(n):
    return tuple((i, i + d) for d in (n >> k for k in range(1, n.bit_length())) for i in range(n) if not i & d)


def _top16_rows(slabs):
    n = PEER_TOPK
    v = _compare_exchange(slabs, _batcher_pairs(len(slabs)))
    for shift in (4, 2, 1):
        if len(v) < n:
            v = v + [pltpu.roll(x, shift, axis=0) for x in reversed(v)]
        else:
            v = [jnp.maximum(v[i], pltpu.roll(v[n - 1 - i], shift, axis=0)) for i in range(n)]
        v = _compare_exchange(v, _bitonic_pairs(n))
    return v


def _peer_kernel(*refs, tile, eb):
    (h_ref, g_ref, wq_ref, sk_ref, u_ref, vt_ref,
     o_ref, xnt_ref, e2_ref, e1s_ref, g16_ref, tv_ref, cand_ref, candn_ref,
     pre_ref, act_ref, acc_ref) = refs
    s = pl.program_id(1)

    @pl.when(s == 0)
    def _():
        xn = _rms(h_ref[...], g_ref[...])
        xnt = xn.T
        xnt_ref[...] = xnt.astype(BF16)
        qt = _dot(wq_ref[...], xnt_ref[...]).astype(BF16)
        cand_ref[N_CAND:, :] = jnp.full((N_CAND_PAD - N_CAND, tile), -1.0, F32)
        candn_ref[N_CAND:, :] = jnp.full((N_CAND_PAD - N_CAND, tile), -1.0, F32)
        for hd in range(PEER_HEADS):
            es = []
            for p in range(2):
                base = (hd * 2 + p) * PEER_HALF
                sc = _dot(sk_ref[hd * 2 + p], qt[base:base + PEER_HALF, :])
                e = jnp.exp(sc - jnp.max(sc, axis=0, keepdims=True))
                es.append(e)
                top = _top16_rows([e[8 * i:8 * i + 8, :] for i in range(PEER_NKEYS // 8)])
                for r in range(PEER_TOPK):
                    tv_ref[p, r:r + 1, :] = top[r][0:1, :]
            for i, (a, b) in enumerate(PEER_PAIRS):
                cand_ref[i:i + 1, :] = tv_ref[0, a:a + 1, :] * tv_ref[1, b:b + 1, :]
            cand = cand_ref[...]
            pad = jnp.full((8, tile), -1.0, F32)
            cslabs = [cand[8 * i:8 * i + 8, :] for i in range(N_CAND_PAD // 8)] + [pad] * (8 - N_CAND_PAD // 8)
            tau = _top16_rows(cslabs)[PEER_TOPK - 1][0:1, :]
            sel = cand >= tau
            inv_z = 0.5 / jnp.sum(jnp.where(sel, cand, 0.0), axis=0, keepdims=True)
            e1s_ref[hd] = es[0] * inv_z
            for lg in range(tile // LANES):
                e2_ref[hd, lg] = es[1][:, lg * LANES:(lg + 1) * LANES]
            t1s = tv_ref[0] * inv_z
            for i, (a, b) in enumerate(PEER_PAIRS):
                candn_ref[i:i + 1, :] = t1s[a:a + 1, :] * tv_ref[1, b:b + 1, :]
            g16_ref[hd] = jnp.min(jnp.where(sel, candn_ref[...], jnp.inf), axis=0, keepdims=True)
        acc_ref[...] = jnp.zeros_like(acc_ref)

    npc = eb // PEER_NKEYS
    nlg = tile // LANES
    rc = PEER_NKEYS // 2
    erf_scale = 1.0 / math.sqrt(2.0)

    pre = _dot(u_ref[...], xnt_ref[...])
    for lg in range(nlg):
        pre_ref[lg] = pre[:, lg * LANES:(lg + 1) * LANES]
    for a in range(npc):
        e1full = [e1s_ref[hd, pl.ds(s * npc + a, 1), :] for hd in range(PEER_HEADS)]
        for lg in range(nlg):
            cols = slice(lg * LANES, (lg + 1) * LANES)
            e1rows = [row[:, cols] for row in e1full]
            g16rows = [g16_ref[hd, :, cols] for hd in range(PEER_HEADS)]
            for r0 in range(0, PEER_NKEYS, rc):
                gate = None
                for hd in range(PEER_HEADS):
                    val = e2_ref[hd, lg, r0:r0 + rc, :] * e1rows[hd]
                    contrib = jnp.where(val >= g16rows[hd], val, 0.0)
                    gate = contrib if gate is None else gate + contrib
                rows = slice(a * PEER_NKEYS + r0, a * PEER_NKEYS + r0 + rc)
                pa = pre_ref[lg, rows, :]
                act_ref[lg, rows, :] = (gate * (pa * (1.0 + lax.erf(pa * erf_scale)))).astype(BF16)
    act = jnp.concatenate([act_ref[lg] for lg in range(nlg)], axis=1)
    acc_ref[...] += _dot(vt_ref[...], act)

    @pl.when(s == pl.num_programs(1) - 1)
    def _():
        o_ref[...] = h_ref[...] + acc_ref[...].T


def _peer_tables(peer_u, peer_v):
    depth, n_exp, _ = peer_u.shape
    vt = peer_v.reshape(depth, n_exp // EB_PEER, EB_PEER, D_MODEL).transpose(0, 1, 3, 2)
    return peer_u.astype(BF16), vt.astype(BF16)


def _peer(h, g, wqt, sk, u, vt, layer):
    n = h.shape[0]
    tile, eb = T_PEER, EB_PEER
    ns = u.shape[1] // eb
    dq = wqt.shape[0]
    return pl.pallas_call(
        functools.partial(_peer_kernel, tile=tile, eb=eb),
        grid=(n // tile, ns),
        in_specs=[
            pl.BlockSpec((tile, D_MODEL), lambda i, s: (i, 0)),
            pl.BlockSpec((1, D_MODEL), lambda i, s: (0, 0)),
            pl.BlockSpec((dq, D_MODEL), lambda i, s: (0, 0)),
            pl.BlockSpec((2 * PEER_HEADS, PEER_NKEYS, PEER_HALF), lambda i, s: (0, 0, 0)),
            pl.BlockSpec((None, eb, D_MODEL), lambda i, s: (layer, s, 0)),
            pl.BlockSpec((None, None, D_MODEL, eb), lambda i, s: (layer, s, 0, 0)),
        ],
        out_specs=pl.BlockSpec((tile, D_MODEL), lambda i, s: (i, 0)),
        out_shape=jax.ShapeDtypeStruct((n, D_MODEL), F32),
        scratch_shapes=[
            pltpu.VMEM((D_MODEL, tile), BF16),
            pltpu.VMEM((PEER_HEADS, tile // LANES, PEER_NKEYS, LANES), F32),
            pltpu.VMEM((PEER_HEADS, PEER_NKEYS, tile), F32),
            pltpu.VMEM((PEER_HEADS, 1, tile), F32),
            pltpu.VMEM((2, PEER_TOPK, tile), F32),
            pltpu.VMEM((N_CAND_PAD, tile), F32),
            pltpu.VMEM((N_CAND_PAD, tile), F32),
            pltpu.VMEM((tile // LANES, eb, LANES), F32),
            pltpu.VMEM((tile // LANES, eb, LANES), BF16),
            pltpu.VMEM((D_MODEL, tile), F32),
        ],
        compiler_params=_cparams(("parallel", "arbitrary")),
        name="peer",
    )(h, g, wqt, sk, u, vt)


def _ple_kernel(h_ref, p_ref, g_ref, wg_ref, wp_ref, fg_ref, o_ref, *, final):
    h = h_ref[...]
    gate = jax.nn.sigmoid(_dot(_rms(h, g_ref[...]).astype(BF16), wg_ref[...]))
    out = h + gate * _dot(p_ref[...].astype(BF16), wp_ref[...])
    o_ref[...] = _rms(out, fg_ref[...]) if final else out


def _ple(h, p, layer, g, wg, wp, fg, final):
    n = h.shape[0]
    tm = TM_PROJ
    const = lambda i: (0, 0)
    return pl.pallas_call(
        functools.partial(_ple_kernel, final=final),
        grid=(n // tm,),
        in_specs=[
            pl.BlockSpec((tm, D_MODEL), lambda i: (i, 0)),
            pl.BlockSpec((None, tm, PLE_DIM), lambda i: (layer, i, 0)),
            pl.BlockSpec((1, D_MODEL), const),
            pl.BlockSpec((D_MODEL, D_MODEL), const),
            pl.BlockSpec((PLE_DIM, D_MODEL), const),
            pl.BlockSpec((1, D_MODEL), const),
        ],
        out_specs=pl.BlockSpec((tm, D_MODEL), lambda i: (i, 0)),
        out_shape=jax.ShapeDtypeStruct((n, D_MODEL), F32),
        compiler_params=_cparams(("parallel",)),
        name="ple",
    )(h, p, g, wg, wp, fg)


def _pad_heads(w, width=SLOT):
    k = w.shape[0]
    w = w.reshape(k, N_HEADS, -1)
    return jnp.pad(w, ((0, 0), (0, 0), (0, width - w.shape[2]))).reshape(k, N_HEADS * width)


def _rot_partner(w_rot):
    half = MLA_ROPE // 2
    return jnp.concatenate([-w_rot[..., half:], w_rot[..., :half]], axis=-1)


def _layer_weights(w_in, w_uq, w_ukv):
    offs = [int(o) for o in np.cumsum(IN_SPLITS)[:-1]]
    fq, fk, fv, fl, sq, sk, sv, cq, ckv, kr = jnp.split(w_in, offs, axis=1)
    att_scale = HEAD_DIM ** -0.5
    wb = jnp.concatenate([_pad_heads(fq * att_scale), _pad_heads(fk),
                          sq * att_scale, sk, fv, sv], axis=1).astype(BF16)
    z = lambda c: jnp.zeros((D_MODEL, c), F32)
    kra = jnp.concatenate([z(MLA_NOPE), kr, z(SLOT - MLA_NOPE - MLA_ROPE)], axis=1)
    krb = jnp.concatenate([z(MLA_NOPE), _rot_partner(kr), z(SLOT - MLA_NOPE - MLA_ROPE)], axis=1)
    wf = jnp.concatenate([cq, ckv, kra, krb, fl, z(LANES - N_HEADS)], axis=1).astype(BF16)
    uq = w_uq.reshape(MLA_Q_RANK, N_HEADS, MLA_NOPE + MLA_ROPE)
    zq = jnp.zeros((MLA_Q_RANK, N_HEADS, SLOT - MLA_NOPE - MLA_ROPE), F32)
    wqa = jnp.concatenate([uq, zq], axis=2).reshape(MLA_Q_RANK, N_HEADS * SLOT).astype(BF16)
    wqb = jnp.concatenate([jnp.zeros_like(uq[..., :MLA_NOPE]), _rot_partner(uq[..., MLA_NOPE:]), zq],
                          axis=2).reshape(MLA_Q_RANK, N_HEADS * SLOT).astype(BF16)
    ukv = w_ukv.reshape(MLA_KV_RANK, N_HEADS, MLA_NOPE + MLA_V)
    wk = jnp.pad(ukv[..., :MLA_NOPE], ((0, 0), (0, 0), (0, SLOT - MLA_NOPE))).reshape(MLA_KV_RANK, -1).astype(BF16)
    wv = ukv[..., MLA_NOPE:].reshape(MLA_KV_RANK, -1).astype(BF16)
    return wb, wf, wqa, wqb, wk, wv


def _rope_lane_freqs():
    half = MLA_ROPE // 2
    inv_freq = ROPE_THETA ** (-jnp.arange(half, dtype=F32) / half)
    zeros = lambda c: jnp.zeros((c,), F32)
    return jnp.concatenate([zeros(MLA_NOPE), inv_freq, inv_freq, zeros(SLOT - MLA_NOPE - MLA_ROPE)]).reshape(1, SLOT)


def kernel(x, p, positions, norm_mix_g, w_in, b_forget, mla_q_norm_g, w_uq, mla_kv_norm_g, w_ukv, mix_out_norm_g, w_o, norm_ffn_g, peer_w_query, peer_sub_keys, peer_u, peer_v, w_ple, ple_norm_g, w_ple_gate, final_norm_g):
    batch, s_len, d = x.shape
    depth = p.shape[0]
    n = batch * s_len
    h = x.reshape(n, d)
    pos = positions.reshape(n, 1)
    invf = _rope_lane_freqs()
    u_all, vt_all = _peer_tables(peer_u, peer_v)
    p_all = p.reshape(depth, n, PLE_DIM)
    row = lambda v: v.reshape(1, -1)
    for i in range(depth):
        wb, wf, wqa, wqb, wk, wv = _layer_weights(w_in[i], w_uq[i], w_ukv[i])
        ob, of = _inproj(h, row(norm_mix_g[i]), wb, wf)
        bias = jnp.pad(b_forget[i], (0, LANES - N_HEADS)).reshape(1, LANES)
        fqk = _foxprep(ob, of, bias, batch, s_len)
        y_fox = _softmax_attn(fqk, 0, fqk, 1, ob, OB_FV // (GW // 2), batch, s_len, True)
        y_sb = _sb_attn(ob, batch, s_len)
        mq, mk, mv = _mla_proj(of, pos, invf, row(mla_q_norm_g[i]), row(mla_kv_norm_g[i]), wqa, wqb, wk, wv)
        y_mla = _softmax_attn(mq, 0, mk, 0, mv, 0, batch, s_len, False)
        h = _outproj(y_fox, y_sb, y_mla, row(mix_out_norm_g[i]), w_o[i].astype(BF16), h)
        sk = peer_sub_keys[i].reshape(2 * PEER_HEADS, PEER_NKEYS, PEER_HALF).astype(BF16)
        h = _peer(h, row(norm_ffn_g[i]), peer_w_query[i].T.astype(BF16), sk, u_all, vt_all, i)
        h = _ple(h, p_all, i, row(ple_norm_g[i]), w_ple_gate[i].astype(BF16),
                 w_ple[i].astype(BF16), row(final_norm_g), i == depth - 1)
    return h.reshape(batch, s_len, d)
```

```python
import functools
import math

import numpy as np
import jax
import jax.numpy as jnp
from jax import lax
from jax.experimental import pallas as pl
from jax.experimental.pallas import tpu as pltpu

F32 = jnp.float32
BF16 = jnp.bfloat16

D_MODEL = 1024
HEAD_DIM = 64
N_HEADS = 4
MLA_Q_RANK = 256
MLA_KV_RANK = 128
MLA_NOPE = 64
MLA_ROPE = 32
MLA_V = 128
ROPE_THETA = 10000.0
IN_SPLITS = (256, 256, 256, 4, 256, 256, 256, MLA_Q_RANK, MLA_KV_RANK, MLA_ROPE)
PEER_HEADS = 8
PEER_NKEYS = 128
PEER_HALF = 64
PEER_TOPK = 16
PLE_DIM = 256
EPS = 1e-6

LANES = 128
SLOT = LANES
VMEM_LIMIT = 56 * 1024 * 1024

TM_PROJ = 1024
TQ_SOFTMAX = 512
TQ_SB = 256
TK_SB = LANES
T_PEER = 1024
EB_PEER = 512

GW = N_HEADS * SLOT
PW = N_HEADS * HEAD_DIM
OB_FQ, OB_FK = 0, GW
OB_SQ = 2 * GW
OB_SK, OB_FV, OB_SV, OB_W = OB_SQ + PW, OB_SQ + 2 * PW, OB_SQ + 3 * PW, OB_SQ + 4 * PW
OF_CQ = 0
OF_CKV = OF_CQ + MLA_Q_RANK
OF_KRA = OF_CKV + MLA_KV_RANK
OF_KRB, OF_FL, OF_W = OF_KRA + SLOT, OF_KRA + 2 * SLOT, OF_KRA + 3 * SLOT

PEER_PAIRS = tuple((a, b) for a in range(PEER_TOPK) for b in range(PEER_TOPK) if (a + 1) * (b + 1) <= PEER_TOPK)
N_CAND = len(PEER_PAIRS)
N_CAND_PAD = -(-N_CAND // 8) * 8


def _cparams(sem):
    return pltpu.CompilerParams(dimension_semantics=sem, vmem_limit_bytes=VMEM_LIMIT)


def _rms(x, g):
    return x * lax.rsqrt(jnp.mean(x * x, axis=-1, keepdims=True) + EPS) * g


def _dot(a, b):
    return jnp.dot(a, b, preferred_element_type=F32)


def _dot_nt(a, b):
    return lax.dot_general(a, b, (((1,), (1,)), ((), ())), preferred_element_type=F32)


def _inproj_kernel(x_ref, g_ref, wb_ref, wf_ref, ob_ref, of_ref):
    xb = _rms(x_ref[...], g_ref[...]).astype(BF16)
    ob_ref[...] = _dot(xb, wb_ref[...]).astype(BF16)
    of_ref[...] = _dot(xb, wf_ref[...])


def _inproj(h, g, wb, wf):
    n = h.shape[0]
    tm = TM_PROJ
    return pl.pallas_call(
        _inproj_kernel,
        grid=(n // tm,),
        in_specs=[
            pl.BlockSpec((tm, D_MODEL), lambda i: (i, 0)),
            pl.BlockSpec((1, D_MODEL), lambda i: (0, 0)),
            pl.BlockSpec((D_MODEL, OB_W), lambda i: (0, 0)),
            pl.BlockSpec((D_MODEL, OF_W), lambda i: (0, 0)),
        ],
        out_specs=[
            pl.BlockSpec((tm, OB_W), lambda i: (i, 0)),
            pl.BlockSpec((tm, OF_W), lambda i: (i, 0)),
        ],
        out_shape=[jax.ShapeDtypeStruct((n, OB_W), BF16), jax.ShapeDtypeStruct((n, OF_W), F32)],
        compiler_params=_cparams(("parallel",)),
        name="inproj",
    )(h, g, wb, wf)


def _split3(x):
    hi = x.astype(BF16)
    r = x - hi.astype(F32)
    mid = r.astype(BF16)
    lo = (r - mid.astype(F32)).astype(BF16)
    return hi, mid, lo


FOX_AUG_LANE = HEAD_DIM


def _fox_placement():
    place = np.zeros((3 * LANES, 2 * GW), np.float32)
    ones = np.zeros((1, 2 * GW), np.float32)
    for h in range(N_HEADS):
        for t in range(3):
            place[t * LANES + h, h * SLOT + FOX_AUG_LANE + t] = 1.0
            place[t * LANES + h, GW + h * SLOT + FOX_AUG_LANE + 3 + t] = -1.0
            ones[0, h * SLOT + FOX_AUG_LANE + 3 + t] = 1.0
            ones[0, GW + h * SLOT + FOX_AUG_LANE + t] = 1.0
    return jnp.asarray(place, BF16), jnp.asarray(ones, F32)


def _foxprep_kernel(qk_ref, fl_ref, b_ref, place_ref, ones_ref, o_ref, f_ref, *, rb):
    s_len = fl_ref.shape[0]
    x = fl_ref[...] + b_ref[...]
    logf = jnp.minimum(x, 0.0) - jnp.log(1.0 + jnp.exp(-jnp.abs(x)))
    row = lax.broadcasted_iota(jnp.int32, (rb, rb), 0)
    col = lax.broadcasted_iota(jnp.int32, (rb, rb), 1)
    tri = jnp.where(col <= row, 1.0, 0.0).astype(BF16)
    carry = jnp.zeros((1, LANES), F32)
    for r in range(s_len // rb):
        hi, mid, lo = _split3(logf[r * rb:(r + 1) * rb, :])
        fb = _dot(tri, hi) + _dot(tri, mid) + _dot(tri, lo) + carry
        f_ref[r * rb:(r + 1) * rb, :] = fb
        carry = fb[rb - 1:rb, :]
    terms = jnp.concatenate(_split3(f_ref[...]), axis=1)
    aug = _dot(terms, place_ref[...]) + ones_ref[...]
    o_ref[...] = (qk_ref[...].astype(F32) + aug).astype(BF16)


def _foxprep(ob, of, bias, batch, s_len):
    n = ob.shape[0]
    w = 2 * GW
    place, ones = _fox_placement()
    const = lambda b: (0, 0)
    return pl.pallas_call(
        functools.partial(_foxprep_kernel, rb=256),
        grid=(batch,),
        in_specs=[
            pl.BlockSpec((s_len, w), lambda b: (b, 0)),
            pl.BlockSpec((s_len, LANES), lambda b: (b, OF_FL // LANES)),
            pl.BlockSpec((1, LANES), const),
            pl.BlockSpec((3 * LANES, w), const),
            pl.BlockSpec((1, w), const),
        ],
        out_specs=pl.BlockSpec((s_len, w), lambda b: (b, 0)),
        out_shape=jax.ShapeDtypeStruct((n, w), BF16),
        scratch_shapes=[pltpu.VMEM((s_len, LANES), F32)],
        compiler_params=_cparams(("parallel",)),
        name="foxprep",
    )(ob, of, bias, place, ones)


def _softmax_attn_kernel(q_ref, k_ref, v_ref, o_ref, *, tq, pair_sum):
    qi = pl.program_id(1)
    row = lax.broadcasted_iota(jnp.int32, (tq, tq), 0)
    col = lax.broadcasted_iota(jnp.int32, (tq, tq), 1)
    causal = col <= row
    slots = [slice(h * SLOT, (h + 1) * SLOT) for h in range(N_HEADS)]
    qs = [q_ref[:, hs] for hs in slots]

    ones = jnp.ones((tq, SLOT), BF16)

    def block(j, carry, masked):
        rows = pl.ds(pl.multiple_of(j * tq, tq), tq)
        ss = [_dot_nt(qs[h], k_ref[rows, slots[h]]) for h in range(N_HEADS)]
        ms, alphas, ps = [], [], []
        for h in range(N_HEADS):
            s = jnp.where(causal, ss[h], -1e30) if masked else ss[h]
            m_new = jnp.maximum(carry[h][0], jnp.max(s, axis=1, keepdims=True))
            ms.append(m_new)
            alphas.append(jnp.exp(carry[h][0] - m_new))
            ps.append(jnp.exp(s - m_new).astype(BF16))
        if pair_sum:
            pvs = []
            for pr in range(N_HEADS // 2):
                vx = jnp.concatenate([v_ref[rows, slots[pr]], ones], axis=1)
                both = _dot(jnp.concatenate([ps[2 * pr], ps[2 * pr + 1]], axis=0), vx)
                pvs += [both[:tq], both[tq:]]
        else:
            pvs = [_dot(ps[h], jnp.concatenate([v_ref[rows, slots[h]], ones], axis=1)) for h in range(N_HEADS)]
        return tuple((ms[h], alphas[h] * carry[h][1] + pvs[h][:, SLOT:], alphas[h] * carry[h][2] + pvs[h][:, :SLOT])
                     for h in range(N_HEADS))

    init = tuple((jnp.full((tq, 1), -1e30, F32), jnp.zeros((tq, SLOT), F32), jnp.zeros((tq, SLOT), F32))
                 for _ in range(N_HEADS))
    carry = lax.fori_loop(0, qi, lambda j, c: block(j, c, False), init)
    outs = [acc / l for _, l, acc in block(qi, carry, True)]
    if pair_sum:
        lane = lax.broadcasted_iota(jnp.int32, (tq, SLOT), 1)
        for pr in range(N_HEADS // 2):
            o_ref[:, slots[pr]] = jnp.where(lane < HEAD_DIM, outs[2 * pr], outs[2 * pr + 1])
    else:
        for h in range(N_HEADS):
            o_ref[:, slots[h]] = outs[h]


def _softmax_attn(q_arr, q_col, k_arr, k_col, v_arr, v_col, batch, s_len, pair_sum):
    n = q_arr.shape[0]
    tq = TQ_SOFTMAX
    nq = s_len // tq
    vw = PW if pair_sum else GW
    return pl.pallas_call(
        functools.partial(_softmax_attn_kernel, tq=tq, pair_sum=pair_sum),
        grid=(batch, nq),
        in_specs=[
            pl.BlockSpec((tq, GW), lambda b, i: (b * nq + i, q_col)),
            pl.BlockSpec((s_len, GW), lambda b, i: (b, k_col)),
            pl.BlockSpec((s_len, vw), lambda b, i: (b, v_col)),
        ],
        out_specs=pl.BlockSpec((tq, vw), lambda b, i: (b * nq + i, 0)),
        out_shape=jax.ShapeDtypeStruct((n, vw), F32),
        compiler_params=_cparams(("parallel", "arbitrary")),
        name="fox_attn" if pair_sum else "mla_attn",
    )(q_arr, k_arr, v_arr)


def _sb_attn_kernel(q_ref, k_ref, v_ref, o_ref, *, tq, tk):
    qi = pl.program_id(1)
    nz = tq // tk
    row = lax.broadcasted_iota(jnp.int32, (tq, tk), 0)
    col = lax.broadcasted_iota(jnp.int32, (tq, tk), 1)
    ur = lax.broadcasted_iota(jnp.int32, (tk, 2 * tk), 0)
    uc = lax.broadcasted_iota(jnp.int32, (tk, 2 * tk), 1)
    uo = jnp.where((uc >= tk) | (ur > uc), 1.0, 0.0).astype(BF16)
    uo2 = jnp.concatenate([uo, uo], axis=0)
    slots = [slice(h * SLOT, (h + 1) * SLOT) for h in range(N_HEADS)]
    lane = lax.broadcasted_iota(jnp.int32, (tq, SLOT), 1)
    own = (lane < HEAD_DIM, lane >= HEAD_DIM)
    qs = [jnp.where(own[h % 2], q_ref[:, slots[h // 2]].astype(F32), 0.0).astype(BF16) for h in range(N_HEADS)]

    def blocks(js, carry, zones):
        cs, accs = list(carry[0]), list(carry[1])
        rows = [pl.ds(pl.multiple_of(j * tk, tk), tk) for j in js]
        stricts = [None if zn is None else (col + zn * tk) < row for zn in zones]
        lbs, hls = [], []
        for b in range(len(js)):
            for h in range(N_HEADS):
                z = _dot_nt(qs[h], k_ref[rows[b], slots[h // 2]])
                lb = jnp.minimum(z, 0.0) - jnp.log(1.0 + jnp.exp(-jnp.abs(z)))
                lom = lb - z
                if stricts[b] is not None:
                    lom = jnp.where(stricts[b], lom, 0.0)
                hi = lom.astype(BF16)
                lo = (lom - hi.astype(F32)).astype(BF16)
                lbs.append(lb)
                hls.append(jnp.concatenate([hi, lo], axis=1))
        r = _dot(jnp.concatenate(hls, axis=0), uo2)
        for b in range(len(js)):
            ws = []
            for h in range(N_HEADS):
                rh = r[(b * N_HEADS + h) * tq:(b * N_HEADS + h + 1) * tq]
                w = jnp.exp(lbs[b * N_HEADS + h] + rh[:, :tk] + cs[h])
                if stricts[b] is not None:
                    w = jnp.where(stricts[b], w, 0.0)
                ws.append(w.astype(BF16))
                cs[h] = cs[h] + rh[:, tk:]
            for pr in range(N_HEADS // 2):
                both = _dot(jnp.concatenate([ws[2 * pr], ws[2 * pr + 1]], axis=0), v_ref[rows[b], slots[pr]])
                accs[pr] = accs[pr] + jnp.where(lane < HEAD_DIM, both[:tq], both[tq:])
        return tuple(cs), tuple(accs)

    carry = (tuple(jnp.zeros((tq, tk), F32) for _ in range(N_HEADS)),
             tuple(jnp.zeros((tq, SLOT), F32) for _ in range(N_HEADS // 2)))
    zone = list(reversed(range(nz)))
    carry = blocks([qi * nz + zi for zi in zone], carry, zone)
    n_full = qi * nz
    per = 2 * nz

    def sweep(first, count):
        return lambda jj, cr: blocks([first - jj * count - i for i in range(count)], cr, [None] * count)

    n_main = n_full // per
    carry = lax.fori_loop(0, n_main, sweep(n_full - 1, per), carry)
    carry = lax.fori_loop(0, (n_full - n_main * per) // nz, sweep(n_full - 1 - n_main * per, nz), carry)
    for pr in range(N_HEADS // 2):
        o_ref[:, slots[pr]] = carry[1][pr]


def _sb_attn(ob, batch, s_len):
    n = ob.shape[0]
    tq, tk = TQ_SB, TK_SB
    nq = s_len // tq
    vw = PW
    return pl.pallas_call(
        functools.partial(_sb_attn_kernel, tq=tq, tk=tk),
        grid=(batch, nq),
        in_specs=[
            pl.BlockSpec((tq, vw), lambda b, i: (b * nq + i, OB_SQ // vw)),
            pl.BlockSpec((s_len, vw), lambda b, i: (b, OB_SK // vw)),
            pl.BlockSpec((s_len, vw), lambda b, i: (b, OB_SV // vw)),
        ],
        out_specs=pl.BlockSpec((tq, vw), lambda b, i: (b * nq + i, 0)),
        out_shape=jax.ShapeDtypeStruct((n, vw), F32),
        compiler_params=_cparams(("parallel", "arbitrary")),
        name="sb_attn",
    )(ob, ob, ob)


def _mla_proj_kernel(cq_ref, ckv_ref, kra_ref, krb_ref, pos_ref, invf_ref, gq_ref, gkv_ref,
                     wqa_ref, wqb_ref, wk_ref, wv_ref, q_ref, k_ref, v_ref, *, scale):
    ang = pos_ref[...].astype(F32) * invf_ref[...]
    cs, sn = jnp.cos(ang), jnp.sin(ang)
    cqn = _rms(cq_ref[...], gq_ref[...]).astype(BF16)
    ckvn = _rms(ckv_ref[...], gkv_ref[...]).astype(BF16)
    qa, qb = _dot(cqn, wqa_ref[...]), _dot(cqn, wqb_ref[...])
    kn = _dot(ckvn, wk_ref[...])
    v_ref[...] = _dot(ckvn, wv_ref[...]).astype(BF16)
    kr = kra_ref[...] * cs + krb_ref[...] * sn
    for h in range(N_HEADS):
        hs = slice(h * SLOT, (h + 1) * SLOT)
        q_ref[:, hs] = ((qa[:, hs] * cs + qb[:, hs] * sn) * scale).astype(BF16)
        k_ref[:, hs] = (kn[:, hs] + kr).astype(BF16)


def _mla_proj(of, pos, invf, gq, gkv, wqa, wqb, wk, wv):
    n = of.shape[0]
    tm = TM_PROJ
    w = N_HEADS * SLOT
    const = lambda i: (0, 0)
    return pl.pallas_call(
        functools.partial(_mla_proj_kernel, scale=(MLA_NOPE + MLA_ROPE) ** -0.5),
        grid=(n // tm,),
        in_specs=[
            pl.BlockSpec((tm, MLA_Q_RANK), lambda i: (i, OF_CQ // MLA_Q_RANK)),
            pl.BlockSpec((tm, LANES), lambda i: (i, OF_CKV // LANES)),
            pl.BlockSpec((tm, LANES), lambda i: (i, OF_KRA // LANES)),
            pl.BlockSpec((tm, LANES), lambda i: (i, OF_KRB // LANES)),
            pl.BlockSpec((tm, 1), lambda i: (i, 0)),
            pl.BlockSpec((1, LANES), const),
            pl.BlockSpec((1, MLA_Q_RANK), const),
            pl.BlockSpec((1, MLA_KV_RANK), const),
            pl.BlockSpec((MLA_Q_RANK, w), const),
            pl.BlockSpec((MLA_Q_RANK, w), const),
            pl.BlockSpec((MLA_KV_RANK, w), const),
            pl.BlockSpec((MLA_KV_RANK, w), const),
        ],
        out_specs=[pl.BlockSpec((tm, w), lambda i: (i, 0))] * 3,
        out_shape=[jax.ShapeDtypeStruct((n, w), BF16)] * 3,
        compiler_params=_cparams(("parallel",)),
        name="mla_proj",
    )(of, of, of, of, pos, invf, gq, gkv, wqa, wqb, wk, wv)


def _outproj_kernel(yf_ref, ys_ref, ym_ref, g_ref, wo_ref, h_ref, o_ref):
    gw = yf_ref.shape[1]
    nf = _rms(yf_ref[...], g_ref[:, 0:gw]).astype(BF16)
    ns = _rms(ys_ref[...], g_ref[:, gw:2 * gw]).astype(BF16)
    nm = _rms(ym_ref[...], g_ref[:, 2 * gw:]).astype(BF16)
    acc = _dot(nf, wo_ref[0:gw, :]) + _dot(ns, wo_ref[gw:2 * gw, :]) + _dot(nm, wo_ref[2 * gw:, :])
    o_ref[...] = h_ref[...] + acc


def _outproj(yf, ys, ym, g, wo, h):
    n = h.shape[0]
    tm = TM_PROJ
    const = lambda i: (0, 0)
    return pl.pallas_call(
        _outproj_kernel,
        grid=(n // tm,),
        in_specs=[
            pl.BlockSpec((tm, yf.shape[1]), lambda i: (i, 0)),
            pl.BlockSpec((tm, ys.shape[1]), lambda i: (i, 0)),
            pl.BlockSpec((tm, ym.shape[1]), lambda i: (i, 0)),
            pl.BlockSpec((1, D_MODEL), const),
            pl.BlockSpec((D_MODEL, D_MODEL), const),
            pl.BlockSpec((tm, D_MODEL), lambda i: (i, 0)),
        ],
        out_specs=pl.BlockSpec((tm, D_MODEL), lambda i: (i, 0)),
        out_shape=jax.ShapeDtypeStruct((n, D_MODEL), F32),
        compiler_params=_cparams(("parallel",)),
        name="outproj",
    )(yf, ys, ym, g, wo, h)


def _batcher_pairs(n):
    def merge(lo, hi, r):
        step = r * 2
        if step < hi - lo:
            yield from merge(lo, hi, step)
            yield from merge(lo + r, hi, step)
            yield from ((i, i + r) for i in range(lo + r, hi - r, step))
        else:
            yield (lo, lo + r)

    def sort(lo, hi):
        if hi - lo >= 1:
            mid = lo + (hi - lo) // 2
            yield from sort(lo, mid)
            yield from sort(mid + 1, hi)
            yield from merge(lo, hi, 1)

    return tuple(sort(0, n - 1))


def _compare_exchange(v, pairs):
    v = list(v)
    for i, j in pairs:
        v[i], v[j] = jnp.maximum(v[i], v[j]), jnp.minimum(v[i], v[j])
    return v


def _bitonic_pairs(n):
    return tuple((i, i + d) for d in (n >> k for k in range(1, n.bit_length())) for i in range(n) if not i & d)


def _top16_rows(slabs):
    n = PEER_TOPK
    v = _compare_exchange(slabs, _batcher_pairs(len(slabs)))
    for shift in (4, 2, 1):
        if len(v) < n:
            v = v + [pltpu.roll(x, shift, axis=0) for x in reversed(v)]
        else:
            v = [jnp.maximum(v[i], pltpu.roll(v[n - 1 - i], shift, axis=0)) for i in range(n)]
        v = _compare_exchange(v, _bitonic_pairs(n))
    return v


def _peer_kernel(*refs, tile, eb):
    (h_ref, g_ref, wq_ref, sk_ref, u_ref, vt_ref,
     o_ref, xnt_ref, e2_ref, e1s_ref, g16_ref, tv_ref, cand_ref, candn_ref,
     pre_ref, act_ref, acc_ref) = refs
    s = pl.program_id(1)

    @pl.when(s == 0)
    def _():
        xn = _rms(h_ref[...], g_ref[...])
        xnt = xn.T
        xnt_ref[...] = xnt.astype(BF16)
        qt = _dot(wq_ref[...], xnt_ref[...]).astype(BF16)
        cand_ref[N_CAND:, :] = jnp.full((N_CAND_PAD - N_CAND, tile), -1.0, F32)
        candn_ref[N_CAND:, :] = jnp.full((N_CAND_PAD - N_CAND, tile), -1.0, F32)
        for hd in range(PEER_HEADS):
            es = []
            for p in range(2):
                base = (hd * 2 + p) * PEER_HALF
                sc = _dot(sk_ref[hd * 2 + p], qt[base:base + PEER_HALF, :])
                e = jnp.exp(sc - jnp.max(sc, axis=0, keepdims=True))
                es.append(e)
                top = _top16_rows([e[8 * i:8 * i + 8, :] for i in range(PEER_NKEYS // 8)])
                for r in range(PEER_TOPK):
                    tv_ref[p, r:r + 1, :] = top[r][0:1, :]
            for i, (a, b) in enumerate(PEER_PAIRS):
                cand_ref[i:i + 1, :] = tv_ref[0, a:a + 1, :] * tv_ref[1, b:b + 1, :]
            cand = cand_ref[...]
            pad = jnp.full((8, tile), -1.0, F32)
            cslabs = [cand[8 * i:8 * i + 8, :] for i in range(N_CAND_PAD // 8)] + [pad] * (8 - N_CAND_PAD // 8)
            tau = _top16_rows(cslabs)[PEER_TOPK - 1][0:1, :]
            sel = cand >= tau
            inv_z = 0.5 / jnp.sum(jnp.where(sel, cand, 0.0), axis=0, keepdims=True)
            e1s_ref[hd] = es[0] * inv_z
            for lg in range(tile // LANES):
                e2_ref[hd, lg] = es[1][:, lg * LANES:(lg + 1) * LANES]
            t1s = tv_ref[0] * inv_z
            for i, (a, b) in enumerate(PEER_PAIRS):
                candn_ref[i:i + 1, :] = t1s[a:a + 1, :] * tv_ref[1, b:b + 1, :]
            g16_ref[hd] = jnp.min(jnp.where(sel, candn_ref[...], jnp.inf), axis=0, keepdims=True)
        acc_ref[...] = jnp.zeros_like(acc_ref)

    npc = eb // PEER_NKEYS
    nlg = tile // LANES
    rc = PEER_NKEYS // 2
    erf_scale = 1.0 / math.sqrt(2.0)

    pre = _dot(u_ref[...], xnt_ref[...])
    for lg in range(nlg):
        pre_ref[lg] = pre[:, lg * LANES:(lg + 1) * LANES]
    for a in range(npc):
        e1full = [e1s_ref[hd, pl.ds(s * npc + a, 1), :] for hd in range(PEER_HEADS)]
        for lg in range(nlg):
            cols = slice(lg * LANES, (lg + 1) * LANES)
            e1rows = [row[:, cols] for row in e1full]
            g16rows = [g16_ref[hd, :, cols] for hd in range(PEER_HEADS)]
            for r0 in range(0, PEER_NKEYS, rc):
                gate = None
                for hd in range(PEER_HEADS):
                    val = e2_ref[hd, lg, r0:r0 + rc, :] * e1rows[hd]
                    contrib = jnp.where(val >= g16rows[hd], val, 0.0)
                    gate = contrib if gate is None else gate + contrib
                rows = slice(a * PEER_NKEYS + r0, a * PEER_NKEYS + r0 + rc)
                pa = pre_ref[lg, rows, :]
                act_ref[lg, rows, :] = (gate * (pa * (1.0 + lax.erf(pa * erf_scale)))).astype(BF16)
    act = jnp.concatenate([act_ref[lg] for lg in range(nlg)], axis=1)
    acc_ref[...] += _dot(vt_ref[...], act)

    @pl.when(s == pl.num_programs(1) - 1)
    def _():
        o_ref[...] = h_ref[...] + acc_ref[...].T


def _peer_tables(peer_u, peer_v):
    depth, n_exp, _ = peer_u.shape
    vt = peer_v.reshape(depth, n_exp // EB_PEER, EB_PEER, D_MODEL).transpose(0, 1, 3, 2)
    return peer_u.astype(BF16), vt.astype(BF16)


def _peer(h, g, wqt, sk, u, vt, layer):
    n = h.shape[0]
    tile, eb = T_PEER, EB_PEER
    ns = u.shape[1] // eb
    dq = wqt.shape[0]
    return pl.pallas_call(
        functools.partial(_peer_kernel, tile=tile, eb=eb),
        grid=(n // tile, ns),
        in_specs=[
            pl.BlockSpec((tile, D_MODEL), lambda i, s: (i, 0)),
            pl.BlockSpec((1, D_MODEL), lambda i, s: (0, 0)),
            pl.BlockSpec((dq, D_MODEL), lambda i, s: (0, 0)),
            pl.BlockSpec((2 * PEER_HEADS, PEER_NKEYS, PEER_HALF), lambda i, s: (0, 0, 0)),
            pl.BlockSpec((None, eb, D_MODEL), lambda i, s: (layer, s, 0)),
            pl.BlockSpec((None, None, D_MODEL, eb), lambda i, s: (layer, s, 0, 0)),
        ],
        out_specs=pl.BlockSpec((tile, D_MODEL), lambda i, s: (i, 0)),
        out_shape=jax.ShapeDtypeStruct((n, D_MODEL), F32),
        scratch_shapes=[
            pltpu.VMEM((D_MODEL, tile), BF16),
            pltpu.VMEM((PEER_HEADS, tile // LANES, PEER_NKEYS, LANES), F32),
            pltpu.VMEM((PEER_HEADS, PEER_NKEYS, tile), F32),
            pltpu.VMEM((PEER_HEADS, 1, tile), F32),
            pltpu.VMEM((2, PEER_TOPK, tile), F32),
            pltpu.VMEM((N_CAND_PAD, tile), F32),
            pltpu.VMEM((N_CAND_PAD, tile), F32),
            pltpu.VMEM((tile // LANES, eb, LANES), F32),
            pltpu.VMEM((tile // LANES, eb, LANES), BF16),
            pltpu.VMEM((D_MODEL, tile), F32),
        ],
        compiler_params=_cparams(("parallel", "arbitrary")),
        name="peer",
    )(h, g, wqt, sk, u, vt)


def _ple_kernel(h_ref, p_ref, g_ref, wg_ref, wp_ref, fg_ref, o_ref, *, final):
    h = h_ref[...]
    gate = jax.nn.sigmoid(_dot(_rms(h, g_ref[...]).astype(BF16), wg_ref[...]))
    out = h + gate * _dot(p_ref[...].astype(BF16), wp_ref[...])
    o_ref[...] = _rms(out, fg_ref[...]) if final else out


def _ple(h, p, layer, g, wg, wp, fg, final):
    n = h.shape[0]
    tm = TM_PROJ
    const = lambda i: (0, 0)
    return pl.pallas_call(
        functools.partial(_ple_kernel, final=final),
        grid=(n // tm,),
        in_specs=[
            pl.BlockSpec((tm, D_MODEL), lambda i: (i, 0)),
            pl.BlockSpec((None, tm, PLE_DIM), lambda i: (layer, i, 0)),
            pl.BlockSpec((1, D_MODEL), const),
            pl.BlockSpec((D_MODEL, D_MODEL), const),
            pl.BlockSpec((PLE_DIM, D_MODEL), const),
            pl.BlockSpec((1, D_MODEL), const),
        ],
        out_specs=pl.BlockSpec((tm, D_MODEL), lambda i: (i, 0)),
        out_shape=jax.ShapeDtypeStruct((n, D_MODEL), F32),
        compiler_params=_cparams(("parallel",)),
        name="ple",
    )(h, p, g, wg, wp, fg)


def _pad_heads(w, width=SLOT):
    k = w.shape[0]
    w = w.reshape(k, N_HEADS, -1)
    return jnp.pad(w, ((0, 0), (0, 0), (0, width - w.shape[2]))).reshape(k, N_HEADS * width)


def _rot_partner(w_rot):
    half = MLA_ROPE // 2
    return jnp.concatenate([-w_rot[..., half:], w_rot[..., :half]], axis=-1)


def _layer_weights(w_in, w_uq, w_ukv):
    offs = [int(o) for o in np.cumsum(IN_SPLITS)[:-1]]
    fq, fk, fv, fl, sq, sk, sv, cq, ckv, kr = jnp.split(w_in, offs, axis=1)
    att_scale = HEAD_DIM ** -0.5
    wb = jnp.concatenate([_pad_heads(fq * att_scale), _pad_heads(fk),
                          sq * att_scale, sk, fv, sv], axis=1).astype(BF16)
    z = lambda c: jnp.zeros((D_MODEL, c), F32)
    kra = jnp.concatenate([z(MLA_NOPE), kr, z(SLOT - MLA_NOPE - MLA_ROPE)], axis=1)
    krb = jnp.concatenate([z(MLA_NOPE), _rot_partner(kr), z(SLOT - MLA_NOPE - MLA_ROPE)], axis=1)
    wf = jnp.concatenate([cq, ckv, kra, krb, fl, z(LANES - N_HEADS)], axis=1).astype(BF16)
    uq = w_uq.reshape(MLA_Q_RANK, N_HEADS, MLA_NOPE + MLA_ROPE)
    zq = jnp.zeros((MLA_Q_RANK, N_HEADS, SLOT - MLA_NOPE - MLA_ROPE), F32)
    wqa = jnp.concatenate([uq, zq], axis=2).reshape(MLA_Q_RANK, N_HEADS * SLOT).astype(BF16)
    wqb = jnp.concatenate([jnp.zeros_like(uq[..., :MLA_NOPE]), _rot_partner(uq[..., MLA_NOPE:]), zq],
                          axis=2).reshape(MLA_Q_RANK, N_HEADS * SLOT).astype(BF16)
    ukv = w_ukv.reshape(MLA_KV_RANK, N_HEADS, MLA_NOPE + MLA_V)
    wk = jnp.pad(ukv[..., :MLA_NOPE], ((0, 0), (0, 0), (0, SLOT - MLA_NOPE))).reshape(MLA_KV_RANK, -1).astype(BF16)
    wv = ukv[..., MLA_NOPE:].reshape(MLA_KV_RANK, -1).astype(BF16)
    return wb, wf, wqa, wqb, wk, wv


def _rope_lane_freqs():
    half = MLA_ROPE // 2
    inv_freq = ROPE_THETA ** (-jnp.arange(half, dtype=F32) / half)
    zeros = lambda c: jnp.zeros((c,), F32)
    return jnp.concatenate([zeros(MLA_NOPE), inv_freq, inv_freq, zeros(SLOT - MLA_NOPE - MLA_ROPE)]).reshape(1, SLOT)


def kernel(x, p, positions, norm_mix_g, w_in, b_forget, mla_q_norm_g, w_uq, mla_kv_norm_g, w_ukv, mix_out_norm_g, w_o, norm_ffn_g, peer_w_query, peer_sub_keys, peer_u, peer_v, w_ple, ple_norm_g, w_ple_gate, final_norm_g):
    batch, s_len, d = x.shape
    depth = p.shape[0]
    n = batch * s_len
    h = x.reshape(n, d)
    pos = positions.reshape(n, 1)
    invf = _rope_lane_freqs()
    u_all, vt_all = _peer_tables(peer_u, peer_v)
    p_all = p.reshape(depth, n, PLE_DIM)
    row = lambda v: v.reshape(1, -1)
    for i in range(depth):
        wb, wf, wqa, wqb, wk, wv = _layer_weights(w_in[i], w_uq[i], w_ukv[i])
        ob, of = _inproj(h, row(norm_mix_g[i]), wb, wf)
        bias = jnp.pad(b_forget[i], (0, LANES - N_HEADS)).reshape(1, LANES)
        fqk = _foxprep(ob, of, bias, batch, s_len)
        y_fox = _softmax_attn(fqk, 0, fqk, 1, ob, OB_FV // (PW), batch, s_len, True)
        y_sb = _sb_attn(ob, batch, s_len)
        mq, mk, mv = _mla_proj(of, pos, invf, row(mla_q_norm_g[i]), row(mla_kv_norm_g[i]), wqa, wqb, wk, wv)
        y_mla = _softmax_attn(mq, 0, mk, 0, mv, 0, batch, s_len, False)
        h = _outproj(y_fox, y_sb, y_mla, row(mix_out_norm_g[i]), w_o[i].astype(BF16), h)
        sk = peer_sub_keys[i].reshape(2 * PEER_HEADS, PEER_NKEYS, PEER_HALF).astype(BF16)
        h = _peer(h, row(norm_ffn_g[i]), peer_w_query[i].T.astype(BF16), sk, u_all, vt_all, i)
        h = _ple(h, p_all, i, row(ple_norm_g[i]), w_ple_gate[i].astype(BF16),
                 w_ple[i].astype(BF16), row(final_norm_g), i == depth - 1)
    return h.reshape(batch, s_len, d)
```

```python
import functools
import math

import numpy as np
import jax
import jax.numpy as jnp
from jax import lax
from jax.experimental import pallas as pl
from jax.experimental.pallas import tpu as pltpu

F32 = jnp.float32
BF16 = jnp.bfloat16

D_MODEL = 1024
HEAD_DIM = 64
N_HEADS = 4
MLA_Q_RANK = 256
MLA_KV_RANK = 128
MLA_NOPE = 64
MLA_ROPE = 32
MLA_V = 128
ROPE_THETA = 10000.0
IN_SPLITS = (256, 256, 256, 4, 256, 256, 256, MLA_Q_RANK, MLA_KV_RANK, MLA_ROPE)
PEER_HEADS = 8
PEER_NKEYS = 128
PEER_HALF = 64
PEER_TOPK = 16
PLE_DIM = 256
EPS = 1e-6

LANES = 128
SLOT = LANES
VMEM_LIMIT = 56 * 1024 * 1024

TM_PROJ = 1024
TM_INPROJ = 1024
TQ_SOFTMAX = 512
TQ_SB = 256
TK_SB = LANES
T_PEER = 1024
EB_PEER = 1024

GW = N_HEADS * SLOT
PW = N_HEADS * HEAD_DIM
OB_FQ, OB_FK = 0, GW
OB_SQ = 2 * GW
OB_SK, OB_FV, OB_SV, OB_W = OB_SQ + PW, OB_SQ + 2 * PW, OB_SQ + 3 * PW, OB_SQ + 4 * PW
OF_CQ = 0
OF_CKV = OF_CQ + MLA_Q_RANK
OF_KRA = OF_CKV + MLA_KV_RANK
OF_KRB, OF_FL, OF_W = OF_KRA + SLOT, OF_KRA + 2 * SLOT, OF_KRA + 3 * SLOT

PEER_PAIRS = tuple((a, b) for a in range(PEER_TOPK) for b in range(PEER_TOPK) if (a + 1) * (b + 1) <= PEER_TOPK)
N_CAND = len(PEER_PAIRS)
N_CAND_PAD = -(-N_CAND // 8) * 8


def _cparams(sem):
    return pltpu.CompilerParams(dimension_semantics=sem, vmem_limit_bytes=VMEM_LIMIT)


def _rms(x, g):
    return x * lax.rsqrt(jnp.mean(x * x, axis=-1, keepdims=True) + EPS) * g


def _dot(a, b):
    return jnp.dot(a, b, preferred_element_type=F32)


def _dot_nt(a, b):
    return lax.dot_general(a, b, (((1,), (1,)), ((), ())), preferred_element_type=F32)


def _inproj_kernel(x_ref, g_ref, wb_ref, wf_ref, pos_ref, invf_ref, gq_ref, gkv_ref, wqa_ref, wqb_ref, wk_ref, wv_ref,
                   b_ref, place_ref, ones_ref, ob_ref, q_ref, k_ref, v_ref, carry_ref, f_ref, *, scale, tiles_per_seq):
    @pl.when(pl.program_id(0) % tiles_per_seq == 0)
    def _():
        carry_ref[...] = jnp.zeros_like(carry_ref)

    xb = _rms(x_ref[...], g_ref[...]).astype(BF16)
    of = _dot(xb, wf_ref[...])
    aug = _fox_augment(of[:, OF_FL:OF_W], b_ref[...], place_ref[...], ones_ref[...], carry_ref, f_ref, rb=256)
    ob = _dot(xb, wb_ref[...])
    ob_ref[:, :OB_SQ] = (ob[:, :OB_SQ] + aug).astype(BF16)
    ob_ref[:, OB_SQ:] = ob[:, OB_SQ:].astype(BF16)
    ang = pos_ref[...].astype(F32) * invf_ref[...]
    cs, sn = jnp.cos(ang), jnp.sin(ang)
    cqn = _rms(of[:, OF_CQ:OF_CKV], gq_ref[...]).astype(BF16)
    ckvn = _rms(of[:, OF_CKV:OF_KRA], gkv_ref[...]).astype(BF16)
    qa, qb = _dot(cqn, wqa_ref[...]), _dot(cqn, wqb_ref[...])
    kn = _dot(ckvn, wk_ref[...])
    v_ref[...] = _dot(ckvn, wv_ref[...]).astype(BF16)
    kr = of[:, OF_KRA:OF_KRB] * cs + of[:, OF_KRB:OF_FL] * sn
    for h in range(N_HEADS):
        hs = slice(h * SLOT, (h + 1) * SLOT)
        q_ref[:, hs] = ((qa[:, hs] * cs + qb[:, hs] * sn) * scale).astype(BF16)
        k_ref[:, hs] = (kn[:, hs] + kr).astype(BF16)


def _inproj(h, g, wb, wf, pos, invf, gq, gkv, wqa, wqb, wk, wv, bias, s_len):
    n = h.shape[0]
    tm = TM_INPROJ
    place, ones = _fox_placement()
    const = lambda i: (0, 0)
    tile = lambda w: pl.BlockSpec((tm, w), lambda i: (i, 0))
    return pl.pallas_call(
        functools.partial(_inproj_kernel, scale=(MLA_NOPE + MLA_ROPE) ** -0.5, tiles_per_seq=s_len // tm),
        grid=(n // tm,),
        in_specs=[
            tile(D_MODEL),
            pl.BlockSpec((1, D_MODEL), const),
            pl.BlockSpec((D_MODEL, OB_W), const),
            pl.BlockSpec((D_MODEL, OF_W), const),
            tile(1),
            pl.BlockSpec((1, LANES), const),
            pl.BlockSpec((1, MLA_Q_RANK), const),
            pl.BlockSpec((1, MLA_KV_RANK), const),
            pl.BlockSpec((MLA_Q_RANK, GW), const),
            pl.BlockSpec((MLA_Q_RANK, GW), const),
            pl.BlockSpec((MLA_KV_RANK, GW), const),
            pl.BlockSpec((MLA_KV_RANK, GW), const),
            pl.BlockSpec((1, LANES), const),
            pl.BlockSpec((3 * LANES, 2 * GW), const),
            pl.BlockSpec((1, 2 * GW), const),
        ],
        out_specs=[tile(OB_W), tile(GW), tile(GW), tile(GW)],
        out_shape=[jax.ShapeDtypeStruct((n, OB_W), BF16)] + [jax.ShapeDtypeStruct((n, GW), BF16)] * 3,
        scratch_shapes=[pltpu.VMEM((1, LANES), F32), pltpu.VMEM((tm, LANES), F32)],
        compiler_params=_cparams(("arbitrary",)),
        name="inproj",
    )(h, g, wb, wf, pos, invf, gq, gkv, wqa, wqb, wk, wv, bias, place, ones)


def _split3(x):
    hi = x.astype(BF16)
    r = x - hi.astype(F32)
    mid = r.astype(BF16)
    lo = (r - mid.astype(F32)).astype(BF16)
    return hi, mid, lo


FOX_AUG_LANE = HEAD_DIM


def _fox_placement():
    place = np.zeros((3 * LANES, 2 * GW), np.float32)
    ones = np.zeros((1, 2 * GW), np.float32)
    for h in range(N_HEADS):
        for t in range(3):
            place[t * LANES + h, h * SLOT + FOX_AUG_LANE + t] = 1.0
            place[t * LANES + h, GW + h * SLOT + FOX_AUG_LANE + 3 + t] = -1.0
            ones[0, h * SLOT + FOX_AUG_LANE + 3 + t] = 1.0
            ones[0, GW + h * SLOT + FOX_AUG_LANE + t] = 1.0
    return jnp.asarray(place, BF16), jnp.asarray(ones, F32)


def _fox_augment(fl, bias, place, ones, carry_ref, f_ref, rb):
    tm = fl.shape[0]
    x = fl + bias
    logf = jnp.minimum(x, 0.0) - jnp.log(1.0 + jnp.exp(-jnp.abs(x)))
    row = lax.broadcasted_iota(jnp.int32, (rb, rb), 0)
    col = lax.broadcasted_iota(jnp.int32, (rb, rb), 1)
    tri = jnp.where(col <= row, 1.0, 0.0).astype(BF16)
    carry = carry_ref[...]
    for r in range(tm // rb):
        hi, mid, lo = _split3(logf[r * rb:(r + 1) * rb, :])
        fb = _dot(tri, hi) + _dot(tri, mid) + _dot(tri, lo) + carry
        f_ref[r * rb:(r + 1) * rb, :] = fb
        carry = fb[rb - 1:rb, :]
    carry_ref[...] = carry
    terms = jnp.concatenate(_split3(f_ref[...]), axis=1)
    return _dot(terms, place) + ones


def _softmax_attn_kernel(q_ref, k_ref, v_ref, o_ref, *, tq, pair_sum):
    qi = pl.program_id(1)
    row = lax.broadcasted_iota(jnp.int32, (tq, tq), 0)
    col = lax.broadcasted_iota(jnp.int32, (tq, tq), 1)
    causal = col <= row
    slots = [slice(h * SLOT, (h + 1) * SLOT) for h in range(N_HEADS)]
    qs = [q_ref[:, hs] for hs in slots]

    ones = jnp.ones((tq, SLOT), BF16)

    def block(j, carry, masked):
        rows = pl.ds(pl.multiple_of(j * tq, tq), tq)
        ss = [_dot_nt(qs[h], k_ref[rows, slots[h]]) for h in range(N_HEADS)]
        ms, alphas, ps = [], [], []
        for h in range(N_HEADS):
            s = jnp.where(causal, ss[h], -1e30) if masked else ss[h]
            m_new = jnp.maximum(carry[h][0], jnp.max(s, axis=1, keepdims=True))
            ms.append(m_new)
            alphas.append(jnp.exp(carry[h][0] - m_new))
            ps.append(jnp.exp(s - m_new).astype(BF16))
        if pair_sum:
            pvs = []
            for pr in range(N_HEADS // 2):
                vx = jnp.concatenate([v_ref[rows, slots[pr]], ones], axis=1)
                both = _dot(jnp.concatenate([ps[2 * pr], ps[2 * pr + 1]], axis=0), vx)
                pvs += [both[:tq], both[tq:]]
        else:
            pvs = [_dot(ps[h], jnp.concatenate([v_ref[rows, slots[h]], ones], axis=1)) for h in range(N_HEADS)]
        return tuple((ms[h], alphas[h] * carry[h][1] + pvs[h][:, SLOT:], alphas[h] * carry[h][2] + pvs[h][:, :SLOT])
                     for h in range(N_HEADS))

    init = tuple((jnp.full((tq, 1), -1e30, F32), jnp.zeros((tq, SLOT), F32), jnp.zeros((tq, SLOT), F32))
                 for _ in range(N_HEADS))
    carry = lax.fori_loop(0, qi, lambda j, c: block(j, c, False), init)
    outs = [acc / l for _, l, acc in block(qi, carry, True)]
    if pair_sum:
        lane = lax.broadcasted_iota(jnp.int32, (tq, SLOT), 1)
        for pr in range(N_HEADS // 2):
            o_ref[:, slots[pr]] = jnp.where(lane < HEAD_DIM, outs[2 * pr], outs[2 * pr + 1])
    else:
        for h in range(N_HEADS):
            o_ref[:, slots[h]] = outs[h]


def _softmax_attn(q_arr, q_col, k_arr, k_col, v_arr, v_col, batch, s_len, pair_sum):
    n = q_arr.shape[0]
    tq = TQ_SOFTMAX
    nq = s_len // tq
    vw = PW if pair_sum else GW
    return pl.pallas_call(
        functools.partial(_softmax_attn_kernel, tq=tq, pair_sum=pair_sum),
        grid=(batch, nq),
        in_specs=[
            pl.BlockSpec((tq, GW), lambda b, i: (b * nq + i, q_col)),
            pl.BlockSpec((s_len, GW), lambda b, i: (b, k_col)),
            pl.BlockSpec((s_len, vw), lambda b, i: (b, v_col)),
        ],
        out_specs=pl.BlockSpec((tq, vw), lambda b, i: (b * nq + i, 0)),
        out_shape=jax.ShapeDtypeStruct((n, vw), F32),
        compiler_params=_cparams(("parallel", "arbitrary")),
        name="fox_attn" if pair_sum else "mla_attn",
    )(q_arr, k_arr, v_arr)


def _sb_attn_kernel(q_ref, k_ref, v_ref, o_ref, *, tq, tk):
    qi = pl.program_id(1)
    nz = tq // tk
    row = lax.broadcasted_iota(jnp.int32, (tq, tk), 0)
    col = lax.broadcasted_iota(jnp.int32, (tq, tk), 1)
    ur = lax.broadcasted_iota(jnp.int32, (tk, 2 * tk), 0)
    uc = lax.broadcasted_iota(jnp.int32, (tk, 2 * tk), 1)
    uo = jnp.where((uc >= tk) | (ur > uc), 1.0, 0.0).astype(BF16)
    uo2 = jnp.concatenate([uo, uo], axis=0)
    slots = [slice(h * SLOT, (h + 1) * SLOT) for h in range(N_HEADS)]
    lane = lax.broadcasted_iota(jnp.int32, (tq, SLOT), 1)
    own = (lane < HEAD_DIM, lane >= HEAD_DIM)
    qs = [jnp.where(own[h % 2], q_ref[:, slots[h // 2]].astype(F32), 0.0).astype(BF16) for h in range(N_HEADS)]

    def blocks(js, carry, zones):
        cs, accs = list(carry[0]), list(carry[1])
        rows = [pl.ds(pl.multiple_of(j * tk, tk), tk) for j in js]
        stricts = [None if zn is None else (col + zn * tk) < row for zn in zones]
        lbs, hls = [], []
        for b in range(len(js)):
            for h in range(N_HEADS):
                z = _dot_nt(qs[h], k_ref[rows[b], slots[h // 2]])
                lb = jnp.minimum(z, 0.0) - jnp.log(1.0 + jnp.exp(-jnp.abs(z)))
                lom = lb - z
                if stricts[b] is not None:
                    lom = jnp.where(stricts[b], lom, 0.0)
                hi = lom.astype(BF16)
                lo = (lom - hi.astype(F32)).astype(BF16)
                lbs.append(lb)
                hls.append(jnp.concatenate([hi, lo], axis=1))
        r = _dot(jnp.concatenate(hls, axis=0), uo2)
        for b in range(len(js)):
            ws = []
            for h in range(N_HEADS):
                rh = r[(b * N_HEADS + h) * tq:(b * N_HEADS + h + 1) * tq]
                w = jnp.exp(lbs[b * N_HEADS + h] + rh[:, :tk] + cs[h])
                if stricts[b] is not None:
                    w = jnp.where(stricts[b], w, 0.0)
                ws.append(w.astype(BF16))
                cs[h] = cs[h] + rh[:, tk:]
            for pr in range(N_HEADS // 2):
                both = _dot(jnp.concatenate([ws[2 * pr], ws[2 * pr + 1]], axis=0), v_ref[rows[b], slots[pr]])
                accs[pr] = accs[pr] + jnp.where(lane < HEAD_DIM, both[:tq], both[tq:])
        return tuple(cs), tuple(accs)

    carry = (tuple(jnp.zeros((tq, tk), F32) for _ in range(N_HEADS)),
             tuple(jnp.zeros((tq, SLOT), F32) for _ in range(N_HEADS // 2)))
    zone = list(reversed(range(nz)))
    carry = blocks([qi * nz + zi for zi in zone], carry, zone)
    n_full = qi * nz
    per = 2 * nz

    def sweep(first, count):
        return lambda jj, cr: blocks([first - jj * count - i for i in range(count)], cr, [None] * count)

    n_main = n_full // per
    carry = lax.fori_loop(0, n_main, sweep(n_full - 1, per), carry)
    carry = lax.fori_loop(0, (n_full - n_main * per) // nz, sweep(n_full - 1 - n_main * per, nz), carry)
    for pr in range(N_HEADS // 2):
        o_ref[:, slots[pr]] = carry[1][pr]


def _sb_attn(ob, batch, s_len):
    n = ob.shape[0]
    tq, tk = TQ_SB, TK_SB
    nq = s_len // tq
    vw = PW
    return pl.pallas_call(
        functools.partial(_sb_attn_kernel, tq=tq, tk=tk),
        grid=(batch, nq),
        in_specs=[
            pl.BlockSpec((tq, vw), lambda b, i: (b * nq + i, OB_SQ // vw)),
            pl.BlockSpec((s_len, vw), lambda b, i: (b, OB_SK // vw)),
            pl.BlockSpec((s_len, vw), lambda b, i: (b, OB_SV // vw)),
        ],
        out_specs=pl.BlockSpec((tq, vw), lambda b, i: (b * nq + i, 0)),
        out_shape=jax.ShapeDtypeStruct((n, vw), F32),
        compiler_params=_cparams(("parallel", "arbitrary")),
        name="sb_attn",
    )(ob, ob, ob)


def _outproj_kernel(yf_ref, ys_ref, ym_ref, g_ref, wo_ref, h_ref, o_ref):
    gw = yf_ref.shape[1]
    nf = _rms(yf_ref[...], g_ref[:, 0:gw]).astype(BF16)
    ns = _rms(ys_ref[...], g_ref[:, gw:2 * gw]).astype(BF16)
    nm = _rms(ym_ref[...], g_ref[:, 2 * gw:]).astype(BF16)
    acc = _dot(nf, wo_ref[0:gw, :]) + _dot(ns, wo_ref[gw:2 * gw, :]) + _dot(nm, wo_ref[2 * gw:, :])
    o_ref[...] = h_ref[...] + acc


def _outproj(yf, ys, ym, g, wo, h):
    n = h.shape[0]
    tm = TM_PROJ
    const = lambda i: (0, 0)
    return pl.pallas_call(
        _outproj_kernel,
        grid=(n // tm,),
        in_specs=[
            pl.BlockSpec((tm, yf.shape[1]), lambda i: (i, 0)),
            pl.BlockSpec((tm, ys.shape[1]), lambda i: (i, 0)),
            pl.BlockSpec((tm, ym.shape[1]), lambda i: (i, 0)),
            pl.BlockSpec((1, D_MODEL), const),
            pl.BlockSpec((D_MODEL, D_MODEL), const),
            pl.BlockSpec((tm, D_MODEL), lambda i: (i, 0)),
        ],
        out_specs=pl.BlockSpec((tm, D_MODEL), lambda i: (i, 0)),
        out_shape=jax.ShapeDtypeStruct((n, D_MODEL), F32),
        compiler_params=_cparams(("parallel",)),
        name="outproj",
    )(yf, ys, ym, g, wo, h)


def _batcher_pairs(n):
    def merge(lo, hi, r):
        step = r * 2
        if step < hi - lo:
            yield from merge(lo, hi, step)
            yield from merge(lo + r, hi, step)
            yield from ((i, i + r) for i in range(lo + r, hi - r, step))
        else:
            yield (lo, lo + r)

    def sort(lo, hi):
        if hi - lo >= 1:
            mid = lo + (hi - lo) // 2
            yield from sort(lo, mid)
            yield from sort(mid + 1, hi)
            yield from merge(lo, hi, 1)

    return tuple(sort(0, n - 1))


def _compare_exchange(v, pairs):
    v = list(v)
    for i, j in pairs:
        v[i], v[j] = jnp.maximum(v[i], v[j]), jnp.minimum(v[i], v[j])
    return v


def _bitonic_pairs(n):
    return tuple((i, i + d) for d in (n >> k for k in range(1, n.bit_length())) for i in range(n) if not i & d)


def _top16_rows(slabs):
    n = PEER_TOPK
    v = _compare_exchange(slabs, _batcher_pairs(len(slabs)))
    for shift in (4, 2, 1):
        if len(v) < n:
            v = v + [pltpu.roll(x, shift, axis=0) for x in reversed(v)]
        else:
            v = [jnp.maximum(v[i], pltpu.roll(v[n - 1 - i], shift, axis=0)) for i in range(n)]
        v = _compare_exchange(v, _bitonic_pairs(n))
    return v


def _peer_kernel(*refs, tile, eb):
    (h_ref, g_ref, wq_ref, sk_ref, u_ref, vt_ref,
     o_ref, xnt_ref, e2_ref, e1s_ref, g16_ref, tv_ref, cand_ref, candn_ref,
     pre_ref, act_ref, acc_ref) = refs
    s = pl.program_id(1)

    @pl.when(s == 0)
    def _():
        xn = _rms(h_ref[...], g_ref[...])
        xnt = xn.T
        xnt_ref[...] = xnt.astype(BF16)
        qt = _dot(wq_ref[...], xnt_ref[...]).astype(BF16)
        cand_ref[N_CAND:, :] = jnp.full((N_CAND_PAD - N_CAND, tile), -1.0, F32)
        candn_ref[N_CAND:, :] = jnp.full((N_CAND_PAD - N_CAND, tile), -1.0, F32)
        for hd in range(PEER_HEADS):
            es = []
            for p in range(2):
                base = (hd * 2 + p) * PEER_HALF
                sc = _dot(sk_ref[hd * 2 + p], qt[base:base + PEER_HALF, :])
                e = jnp.exp(sc - jnp.max(sc, axis=0, keepdims=True))
                es.append(e)
                top = _top16_rows([e[8 * i:8 * i + 8, :] for i in range(PEER_NKEYS // 8)])
                for r in range(PEER_TOPK):
                    tv_ref[p, r:r + 1, :] = top[r][0:1, :]
            for i, (a, b) in enumerate(PEER_PAIRS):
                cand_ref[i:i + 1, :] = tv_ref[0, a:a + 1, :] * tv_ref[1, b:b + 1, :]
            cand = cand_ref[...]
            pad = jnp.full((8, tile), -1.0, F32)
            cslabs = [cand[8 * i:8 * i + 8, :] for i in range(N_CAND_PAD // 8)] + [pad] * (8 - N_CAND_PAD // 8)
            tau = _top16_rows(cslabs)[PEER_TOPK - 1][0:1, :]
            sel = cand >= tau
            inv_z = 0.5 / jnp.sum(jnp.where(sel, cand, 0.0), axis=0, keepdims=True)
            e1s_ref[hd] = es[0] * inv_z
            for lg in range(tile // LANES):
                e2_ref[hd, lg] = es[1][:, lg * LANES:(lg + 1) * LANES]
            t1s = tv_ref[0] * inv_z
            for i, (a, b) in enumerate(PEER_PAIRS):
                candn_ref[i:i + 1, :] = t1s[a:a + 1, :] * tv_ref[1, b:b + 1, :]
            g16_ref[hd] = jnp.min(jnp.where(sel, candn_ref[...], jnp.inf), axis=0, keepdims=True)
        acc_ref[...] = jnp.zeros_like(acc_ref)

    npc = eb // PEER_NKEYS
    nlg = tile // LANES
    rc = PEER_NKEYS // 2
    erf_scale = 1.0 / math.sqrt(2.0)

    pre = _dot(u_ref[...], xnt_ref[...])
    for lg in range(nlg):
        pre_ref[lg] = pre[:, lg * LANES:(lg + 1) * LANES]
    for a in range(npc):
        e1full = [e1s_ref[hd, pl.ds(s * npc + a, 1), :] for hd in range(PEER_HEADS)]
        for lg in range(nlg):
            cols = slice(lg * LANES, (lg + 1) * LANES)
            e1rows = [row[:, cols] for row in e1full]
            g16rows = [g16_ref[hd, :, cols] for hd in range(PEER_HEADS)]
            for r0 in range(0, PEER_NKEYS, rc):
                gate = None
                for hd in range(PEER_HEADS):
                    val = e2_ref[hd, lg, r0:r0 + rc, :] * e1rows[hd]
                    contrib = jnp.where(val >= g16rows[hd], val, 0.0)
                    gate = contrib if gate is None else gate + contrib
                rows = slice(a * PEER_NKEYS + r0, a * PEER_NKEYS + r0 + rc)
                pa = pre_ref[lg, rows, :]
                act_ref[lg, rows, :] = (gate * (pa * (1.0 + lax.erf(pa * erf_scale)))).astype(BF16)
    act = jnp.concatenate([act_ref[lg] for lg in range(nlg)], axis=1)
    acc_ref[...] += _dot(vt_ref[...], act)

    @pl.when(s == pl.num_programs(1) - 1)
    def _():
        o_ref[...] = h_ref[...] + acc_ref[...].T


def _peer_tables(peer_u, peer_v):
    depth, n_exp, _ = peer_u.shape
    vt = peer_v.reshape(depth, n_exp // EB_PEER, EB_PEER, D_MODEL).transpose(0, 1, 3, 2)
    return peer_u.astype(BF16), vt.astype(BF16)


def _peer(h, g, wqt, sk, u, vt, layer):
    n = h.shape[0]
    tile, eb = T_PEER, EB_PEER
    ns = u.shape[1] // eb
    dq = wqt.shape[0]
    return pl.pallas_call(
        functools.partial(_peer_kernel, tile=tile, eb=eb),
        grid=(n // tile, ns),
        in_specs=[
            pl.BlockSpec((tile, D_MODEL), lambda i, s: (i, 0)),
            pl.BlockSpec((1, D_MODEL), lambda i, s: (0, 0)),
            pl.BlockSpec((dq, D_MODEL), lambda i, s: (0, 0)),
            pl.BlockSpec((2 * PEER_HEADS, PEER_NKEYS, PEER_HALF), lambda i, s: (0, 0, 0)),
            pl.BlockSpec((None, eb, D_MODEL), lambda i, s: (layer, s, 0)),
            pl.BlockSpec((None, None, D_MODEL, eb), lambda i, s: (layer, s, 0, 0)),
        ],
        out_specs=pl.BlockSpec((tile, D_MODEL), lambda i, s: (i, 0)),
        out_shape=jax.ShapeDtypeStruct((n, D_MODEL), F32),
        scratch_shapes=[
            pltpu.VMEM((D_MODEL, tile), BF16),
            pltpu.VMEM((PEER_HEADS, tile // LANES, PEER_NKEYS, LANES), F32),
            pltpu.VMEM((PEER_HEADS, PEER_NKEYS, tile), F32),
            pltpu.VMEM((PEER_HEADS, 1, tile), F32),
            pltpu.VMEM((2, PEER_TOPK, tile), F32),
            pltpu.VMEM((N_CAND_PAD, tile), F32),
            pltpu.VMEM((N_CAND_PAD, tile), F32),
            pltpu.VMEM((tile // LANES, eb, LANES), F32),
            pltpu.VMEM((tile // LANES, eb, LANES), BF16),
            pltpu.VMEM((D_MODEL, tile), F32),
        ],
        compiler_params=_cparams(("parallel", "arbitrary")),
        name="peer",
    )(h, g, wqt, sk, u, vt)


def _ple_kernel(h_ref, p_ref, g_ref, wg_ref, wp_ref, fg_ref, o_ref, *, final):
    h = h_ref[...]
    gate = jax.nn.sigmoid(_dot(_rms(h, g_ref[...]).astype(BF16), wg_ref[...]))
    out = h + gate * _dot(p_ref[...].astype(BF16), wp_ref[...])
    o_ref[...] = _rms(out, fg_ref[...]) if final else out


def _ple(h, p, layer, g, wg, wp, fg, final):
    n = h.shape[0]
    tm = TM_PROJ
    const = lambda i: (0, 0)
    return pl.pallas_call(
        functools.partial(_ple_kernel, final=final),
        grid=(n // tm,),
        in_specs=[
            pl.BlockSpec((tm, D_MODEL), lambda i: (i, 0)),
            pl.BlockSpec((None, tm, PLE_DIM), lambda i: (layer, i, 0)),
            pl.BlockSpec((1, D_MODEL), const),
            pl.BlockSpec((D_MODEL, D_MODEL), const),
            pl.BlockSpec((PLE_DIM, D_MODEL), const),
            pl.BlockSpec((1, D_MODEL), const),
        ],
        out_specs=pl.BlockSpec((tm, D_MODEL), lambda i: (i, 0)),
        out_shape=jax.ShapeDtypeStruct((n, D_MODEL), F32),
        compiler_params=_cparams(("parallel",)),
        name="ple",
    )(h, p, g, wg, wp, fg)


def _pad_heads(w, width=SLOT):
    k = w.shape[0]
    w = w.reshape(k, N_HEADS, -1)
    return jnp.pad(w, ((0, 0), (0, 0), (0, width - w.shape[2]))).reshape(k, N_HEADS * width)


def _rot_partner(w_rot):
    half = MLA_ROPE // 2
    return jnp.concatenate([-w_rot[..., half:], w_rot[..., :half]], axis=-1)


def _layer_weights(w_in, w_uq, w_ukv):
    offs = [int(o) for o in np.cumsum(IN_SPLITS)[:-1]]
    fq, fk, fv, fl, sq, sk, sv, cq, ckv, kr = jnp.split(w_in, offs, axis=1)
    att_scale = HEAD_DIM ** -0.5
    wb = jnp.concatenate([_pad_heads(fq * att_scale), _pad_heads(fk),
                          sq * att_scale, sk, fv, sv], axis=1).astype(BF16)
    z = lambda c: jnp.zeros((D_MODEL, c), F32)
    kra = jnp.concatenate([z(MLA_NOPE), kr, z(SLOT - MLA_NOPE - MLA_ROPE)], axis=1)
    krb = jnp.concatenate([z(MLA_NOPE), _rot_partner(kr), z(SLOT - MLA_NOPE - MLA_ROPE)], axis=1)
    wf = jnp.concatenate([cq, ckv, kra, krb, fl, z(LANES - N_HEADS)], axis=1).astype(BF16)
    uq = w_uq.reshape(MLA_Q_RANK, N_HEADS, MLA_NOPE + MLA_ROPE)
    zq = jnp.zeros((MLA_Q_RANK, N_HEADS, SLOT - MLA_NOPE - MLA_ROPE), F32)
    wqa = jnp.concatenate([uq, zq], axis=2).reshape(MLA_Q_RANK, N_HEADS * SLOT).astype(BF16)
    wqb = jnp.concatenate([jnp.zeros_like(uq[..., :MLA_NOPE]), _rot_partner(uq[..., MLA_NOPE:]), zq],
                          axis=2).reshape(MLA_Q_RANK, N_HEADS * SLOT).astype(BF16)
    ukv = w_ukv.reshape(MLA_KV_RANK, N_HEADS, MLA_NOPE + MLA_V)
    wk = jnp.pad(ukv[..., :MLA_NOPE], ((0, 0), (0, 0), (0, SLOT - MLA_NOPE))).reshape(MLA_KV_RANK, -1).astype(BF16)
    wv = ukv[..., MLA_NOPE:].reshape(MLA_KV_RANK, -1).astype(BF16)
    return wb, wf, wqa, wqb, wk, wv


def _rope_lane_freqs():
    half = MLA_ROPE // 2
    inv_freq = ROPE_THETA ** (-jnp.arange(half, dtype=F32) / half)
    zeros = lambda c: jnp.zeros((c,), F32)
    return jnp.concatenate([zeros(MLA_NOPE), inv_freq, inv_freq, zeros(SLOT - MLA_NOPE - MLA_ROPE)]).reshape(1, SLOT)


def kernel(x, p, positions, norm_mix_g, w_in, b_forget, mla_q_norm_g, w_uq, mla_kv_norm_g, w_ukv, mix_out_norm_g, w_o, norm_ffn_g, peer_w_query, peer_sub_keys, peer_u, peer_v, w_ple, ple_norm_g, w_ple_gate, final_norm_g):
    batch, s_len, d = x.shape
    depth = p.shape[0]
    n = batch * s_len
    h = x.reshape(n, d)
    pos = positions.reshape(n, 1)
    invf = _rope_lane_freqs()
    u_all, vt_all = _peer_tables(peer_u, peer_v)
    p_all = p.reshape(depth, n, PLE_DIM)
    row = lambda v: v.reshape(1, -1)
    for i in range(depth):
        wb, wf, wqa, wqb, wk, wv = _layer_weights(w_in[i], w_uq[i], w_ukv[i])
        bias = jnp.pad(b_forget[i], (0, LANES - N_HEADS)).reshape(1, LANES)
        ob, mq, mk, mv = _inproj(h, row(norm_mix_g[i]), wb, wf, pos, invf,
                                 row(mla_q_norm_g[i]), row(mla_kv_norm_g[i]), wqa, wqb, wk, wv, bias, s_len)
        y_fox = _softmax_attn(ob, OB_FQ // GW, ob, OB_FK // GW, ob, OB_FV // PW, batch, s_len, True)
        y_sb = _sb_attn(ob, batch, s_len)
        y_mla = _softmax_attn(mq, 0, mk, 0, mv, 0, batch, s_len, False)
        h = _outproj(y_fox, y_sb, y_mla, row(mix_out_norm_g[i]), w_o[i].astype(BF16), h)
        sk = peer_sub_keys[i].reshape(2 * PEER_HEADS, PEER_NKEYS, PEER_HALF).astype(BF16)
        h = _peer(h, row(norm_ffn_g[i]), peer_w_query[i].T.astype(BF16), sk, u_all, vt_all, i)
        h = _ple(h, p_all, i, row(ple_norm_g[i]), w_ple_gate[i].astype(BF16),
                 w_ple[i].astype(BF16), row(final_norm_g), i == depth - 1)
    return h.reshape(batch, s_len, d)
```
